```python
import math, functools
import jax, jax.numpy as jnp
from jax import lax
import numpy as np

D_MODEL = 2048
BATCH = 8
SEQ = 2048
DEPTH = 1
DEC_BATCH = 128
DEC_SEQ = 4
PAST_LEN = 2048
PAGE_SIZE = 128

MIX_WIDTH = D_MODEL
ATTN_WIDTH = MIX_WIDTH // 2
SGU_WIDTH = MIX_WIDTH - ATTN_WIDTH
HEAD_DIM = 128
N_HEADS = ATTN_WIDTH // HEAD_DIM
SGU_GROUP_DIM = 128
N_SGU_GROUPS = SGU_WIDTH // SGU_GROUP_DIM
PROJ_WIDTH = 3 * ATTN_WIDTH + 2 * SGU_WIDTH
MOBA_BLOCK = 256
MOBA_TOPK = 3
ATTN_QUERY_BLOCK = 16
SGU_CHUNK = 128
N_BUCKETS = 32
MAX_DISTANCE = 128
N_EXPERTS = 32
TOP_K = 4
D_EXPERT = D_MODEL
SWIGLU_ALPHA = 1.702
SWIGLU_LIMIT = 7.0
MOE_BLOCK = 128
RMS_EPS = 1e-5
NEG = -1e30

kernel_name = "hymba_moba_sgu_moe_decode_step"


def rmsnorm(x, g):
    x32 = x.astype(jnp.float32)
    y = x32 * lax.rsqrt(jnp.mean(x32 * x32, axis=-1, keepdims=True) + RMS_EPS)
    return (y * g.astype(jnp.float32)).astype(x.dtype)


def rel_bias(bias_table, dist, head_idx):
    dist = jnp.maximum(dist, 0)
    max_exact = N_BUCKETS // 2
    dist_f = jnp.maximum(dist, max_exact).astype(jnp.float32)
    large = max_exact + (jnp.log(dist_f / max_exact) / math.log(MAX_DISTANCE / max_exact)
                         * (N_BUCKETS - max_exact)).astype(jnp.int32)
    bucket = jnp.where(dist < max_exact, dist, jnp.minimum(large, N_BUCKETS - 1))
    return bias_table[bucket, head_idx].astype(jnp.float32)


def moba_core(q, q_pos, means, n_past, gather_k, gather_v, k_own, v_own, own_pos, bias_table):
    B, Q, H, D = q.shape
    nbm = means.shape[1]
    if nbm < MOBA_TOPK:
        means = jnp.pad(means, ((0, 0), (0, MOBA_TOPK - nbm), (0, 0), (0, 0)))
    blk = jnp.arange(means.shape[1])
    gate = jnp.einsum('bqhd,bnhd->bqhn', q, means).astype(jnp.float32)
    gate = jnp.where(blk < n_past, gate, NEG)
    _, idx = lax.top_k(gate, MOBA_TOPK)
    idx = jnp.minimum(idx, max(nbm, 1) - 1)
    slot_ok = jnp.arange(MOBA_TOPK) < n_past
    scale = HEAD_DIM ** -0.5
    head4 = jnp.arange(H)[None, None, :, None]
    offs = jnp.arange(MOBA_BLOCK)
    logits = []
    for r in range(MOBA_TOPK):
        k_r = gather_k(idx[..., r])
        k_pos = idx[..., r, None] * MOBA_BLOCK + offs
        l_r = jnp.einsum('bqhd,bqhkd->bqhk', q, k_r).astype(jnp.float32) * scale
        l_r = l_r + rel_bias(bias_table, q_pos[None, :, None, None] - k_pos, head4)
        logits.append(jnp.where(slot_ok[r], l_r, NEG))
    dist_own = (q_pos[:, None] - own_pos[None, :])[:, None, :]
    l_own = jnp.einsum('bqhd,bkhd->bqhk', q, k_own).astype(jnp.float32) * scale
    l_own = l_own + rel_bias(bias_table, dist_own, jnp.arange(H)[None, :, None])[None]
    l_own = jnp.where((dist_own >= 0)[None], l_own, NEG)
    logits.append(l_own)
    p = jax.nn.softmax(jnp.concatenate(logits, axis=-1), axis=-1).astype(v_own.dtype)
    out = jnp.einsum('bqhk,bkhd->bqhd', p[..., MOBA_TOPK * MOBA_BLOCK:], v_own)
    for r in range(MOBA_TOPK):
        p_r = p[..., r * MOBA_BLOCK:(r + 1) * MOBA_BLOCK]
        out = out + jnp.einsum('bqhk,bqhkd->bqhd', p_r, gather_v(idx[..., r]))
    return out


def moba_prompt(q, k, v, bias_table):
    B, T, H, D = q.shape
    nb = -(-T // MOBA_BLOCK)
    pad = ((0, 0), (0, nb * MOBA_BLOCK - T), (0, 0), (0, 0))
    k_blocks = jnp.pad(k, pad).reshape(B, nb, MOBA_BLOCK, H, D)
    v_blocks = jnp.pad(v, pad).reshape(B, nb, MOBA_BLOCK, H, D)
    means = jnp.mean(k_blocks, axis=2)
    bi = jnp.arange(B)[:, None, None]
    hi = jnp.arange(H)[None, None, :]
    gather_k = lambda idx: k_blocks[bi, idx, :, hi, :]
    gather_v = lambda idx: v_blocks[bi, idx, :, hi, :]
    nq = T // ATTN_QUERY_BLOCK
    q_chunks = jnp.moveaxis(q.reshape(B, nq, ATTN_QUERY_BLOCK, H, D), 1, 0)

    def one_chunk(args):
        c, q_c = args
        s0 = c * ATTN_QUERY_BLOCK
        j = s0 // MOBA_BLOCK
        k_own = lax.dynamic_index_in_dim(k_blocks, j, axis=1, keepdims=False)
        v_own = lax.dynamic_index_in_dim(v_blocks, j, axis=1, keepdims=False)
        q_pos = s0 + jnp.arange(ATTN_QUERY_BLOCK)
        own_pos = j * MOBA_BLOCK + jnp.arange(MOBA_BLOCK)
        return moba_core(q_c, q_pos, means, j, gather_k, gather_v, k_own, v_own, own_pos, bias_table)

    out = lax.map(one_chunk, (jnp.arange(nq), q_chunks))
    return jnp.moveaxis(out, 0, 1).reshape(B, T, H * D)


def moba_sample(q, k, v, cache_k, cache_v, page_table, bias_table):
    B, S, H, D = q.shape
    past_len = page_table.shape[1] * PAGE_SIZE
    ppb = MOBA_BLOCK // PAGE_SIZE
    nbp = past_len // MOBA_BLOCK
    own_start = nbp * MOBA_BLOCK
    n_own_past = past_len - own_start
    k_full = cache_k[page_table[:, :nbp * ppb]]
    means = jnp.mean(k_full.reshape(B, nbp, MOBA_BLOCK, H, D), axis=2)
    own_pages = page_table[:, own_start // PAGE_SIZE:]
    k_own = jnp.concatenate([cache_k[own_pages].reshape(B, n_own_past, H, D), k], axis=1)
    v_own = jnp.concatenate([cache_v[own_pages].reshape(B, n_own_past, H, D), v], axis=1)
    own_pos = own_start + jnp.arange(n_own_past + S)
    bi = jnp.arange(B)[:, None, None, None]
    hi = jnp.arange(H)[None, None, :, None]

    def page_ids(idx):
        return page_table[bi, idx[..., None] * ppb + jnp.arange(ppb)]

    def gather_k(idx):
        return cache_k[page_ids(idx), :, hi, :].reshape(B, idx.shape[1], H, MOBA_BLOCK, D)

    def gather_v(idx):
        return cache_v[page_ids(idx), :, hi, :].reshape(B, idx.shape[1], H, MOBA_BLOCK, D)

    def one_query(args):
        i, q_i = args
        q_pos = past_len + i + jnp.arange(1)
        return moba_core(q_i[:, None], q_pos, means, nbp, gather_k, gather_v,
                         k_own, v_own, own_pos, bias_table)[:, 0]

    out = lax.map(one_query, (jnp.arange(S), jnp.moveaxis(q, 1, 0)))
    return jnp.moveaxis(out, 0, 1).reshape(B, S, H * D)


def spatial_gating(u, v, w_sp, b_sp):
    B, T, G, C = u.shape
    nc = -(-T // SGU_CHUNK)
    vc = jnp.pad(v, ((0, 0), (0, nc * SGU_CHUNK - T), (0, 0), (0, 0))).reshape(B, nc, SGU_CHUNK, G, C)
    w = w_sp * jnp.tril(jnp.ones((SGU_CHUNK, SGU_CHUNK), w_sp.dtype))
    mixed = jnp.einsum('gts,bnsgc->bntgc', w, vc) + b_sp.T[None, None, :, :, None]
    mixed = mixed.reshape(B, nc * SGU_CHUNK, G, C)[:, :T]
    return u * mixed


def moe_ffn(h, w_router, b_router, w_gate_up, b_gate_up, w_down, b_down):
    B, T, D = h.shape
    x = h.reshape(B * T, D)
    n_tok = x.shape[0]
    logits = (x @ w_router + b_router).astype(jnp.float32)
    top_val, top_idx = lax.top_k(logits, TOP_K)
    gates = jax.nn.softmax(top_val, axis=-1)
    n_assign = n_tok * TOP_K
    e_flat = top_idx.reshape(-1)
    tok_flat = jnp.repeat(jnp.arange(n_tok), TOP_K)
    order = jnp.argsort(e_flat)
    e_sorted = e_flat[order]
    tok_sorted = tok_flat[order]
    gate_sorted = gates.reshape(-1)[order]
    counts = jnp.bincount(e_flat, length=N_EXPERTS)
    padded = (counts + MOE_BLOCK - 1) // MOE_BLOCK * MOE_BLOCK
    pad_end = jnp.cumsum(padded)
    pad_start = pad_end - padded
    raw_start = jnp.cumsum(counts) - counts
    dest = pad_start[e_sorted] + jnp.arange(n_assign) - raw_start[e_sorted]
    n_blocks = n_assign // MOE_BLOCK + N_EXPERTS
    buf = jnp.zeros((n_blocks * MOE_BLOCK, D), x.dtype).at[dest].set(x[tok_sorted])
    block_expert = jnp.minimum(
        jnp.searchsorted(pad_end, jnp.arange(n_blocks) * MOE_BLOCK, side='right'), N_EXPERTS - 1)

    def expert_block(args):
        e, xb = args
        gu = xb @ w_gate_up[e] + b_gate_up[e]
        g, up = jnp.split(gu, 2, axis=-1)
        g = jnp.minimum(g, SWIGLU_LIMIT)
        up = jnp.clip(up, -SWIGLU_LIMIT, SWIGLU_LIMIT)
        act = g * jax.nn.sigmoid(SWIGLU_ALPHA * g) * (up + 1)
        return act @ w_down[e] + b_down[e]

    y_buf = lax.map(expert_block, (block_expert, buf.reshape(n_blocks, MOE_BLOCK, D))).reshape(-1, D)
    y = jax.ops.segment_sum(gate_sorted[:, None] * y_buf[dest], tok_sorted, num_segments=n_tok)
    return y.astype(h.dtype).reshape(B, T, D)


def trunk_layer(x, c, attn_fn, w_ada, b_ada, g_mix, w_in, g_sgu, w_sp, b_sp, g_out_attn, g_out_sgu,
                w_out, g_ffn, w_router, b_router, w_gate_up, b_gate_up, w_down, b_down):
    B, T, _ = x.shape
    mod = (jax.nn.silu(c) @ w_ada + b_ada)[:, None, :]
    sh_m, sc_m, gt_m, sh_f, sc_f, gt_f = jnp.split(mod, 6, axis=-1)
    h = rmsnorm(x, g_mix) * (1 + sc_m) + sh_m
    proj = h @ w_in
    q, k, v, gu, gv = jnp.split(
        proj, [ATTN_WIDTH, 2 * ATTN_WIDTH, 3 * ATTN_WIDTH, 3 * ATTN_WIDTH + SGU_WIDTH], axis=-1)
    q = q.reshape(B, T, N_HEADS, HEAD_DIM)
    k = k.reshape(B, T, N_HEADS, HEAD_DIM)
    v = v.reshape(B, T, N_HEADS, HEAD_DIM)
    u = jax.nn.gelu(gu, approximate=False).reshape(B, T, N_SGU_GROUPS, SGU_GROUP_DIM)
    vs = rmsnorm(jax.nn.gelu(gv, approximate=False).reshape(B, T, N_SGU_GROUPS, SGU_GROUP_DIM), g_sgu)
    attn = attn_fn(q, k, v)
    sgu = spatial_gating(u, vs, w_sp, b_sp).reshape(B, T, SGU_WIDTH)
    merged = jnp.concatenate([rmsnorm(attn, g_out_attn), rmsnorm(sgu, g_out_sgu)], axis=-1)
    x = x + gt_m * (merged @ w_out)
    h = rmsnorm(x, g_ffn) * (1 + sc_f) + sh_f
    x = x + gt_f * moe_ffn(h, w_router, b_router, w_gate_up, b_gate_up, w_down, b_down)
    return x, k, v, vs


def setup_inputs(seed: int = 0) -> dict:
    key = jax.random.key(seed)
    ks = jax.random.split(key, 32)
    n_pages = PAST_LEN // PAGE_SIZE
    n_pool = (DEC_BATCH * n_pages * 5) // 4
    nrm = lambda k, shape, s: jax.random.normal(k, shape, jnp.float32) * s
    one = lambda k, shape: 1.0 + nrm(k, shape, 0.1)
    page_table = jax.random.permutation(ks[6], n_pool)[:DEC_BATCH * n_pages]
    page_table = page_table.reshape(DEC_BATCH, n_pages).astype(jnp.int32)
    return {
        'x_prompt': nrm(ks[0], (BATCH, SEQ, D_MODEL), 1.0),
        'x_sample': nrm(ks[1], (DEC_BATCH, DEC_SEQ, D_MODEL), 1.0),
        'cache_k': nrm(ks[4], (DEPTH, n_pool, PAGE_SIZE, N_HEADS, HEAD_DIM), 1.0),
        'cache_v': nrm(ks[5], (DEPTH, n_pool, PAGE_SIZE, N_HEADS, HEAD_DIM), 1.0),
        'page_table': page_table,
        'c_prompt': nrm(ks[2], (BATCH, D_MODEL), 1.0),
        'c_sample': nrm(ks[3], (DEC_BATCH, D_MODEL), 1.0),
        'rel_bias_table': nrm(ks[7], (N_BUCKETS, N_HEADS), 0.5),
        'w_ada': nrm(ks[8], (DEPTH, D_MODEL, 6 * D_MODEL), 0.5 * D_MODEL ** -0.5),
        'b_ada': nrm(ks[9], (DEPTH, 6 * D_MODEL), 0.02),
        'g_mix': one(ks[10], (DEPTH, D_MODEL)),
        'w_in': nrm(ks[11], (DEPTH, D_MODEL, PROJ_WIDTH), D_MODEL ** -0.5),
        'g_sgu': one(ks[12], (DEPTH, N_SGU_GROUPS, SGU_GROUP_DIM)),
        'w_sp': nrm(ks[13], (DEPTH, N_SGU_GROUPS, SGU_CHUNK, SGU_CHUNK), SGU_CHUNK ** -0.5),
        'b_sp': one(ks[14], (DEPTH, N_SGU_GROUPS, SGU_CHUNK)),
        'g_out_attn': one(ks[15], (DEPTH, ATTN_WIDTH)),
        'g_out_sgu': one(ks[16], (DEPTH, SGU_WIDTH)),
        'w_out': nrm(ks[17], (DEPTH, MIX_WIDTH, D_MODEL), MIX_WIDTH ** -0.5),
        'g_ffn': one(ks[18], (DEPTH, D_MODEL)),
        'w_router': nrm(ks[19], (DEPTH, D_MODEL, N_EXPERTS), D_MODEL ** -0.5),
        'b_router': nrm(ks[20], (DEPTH, N_EXPERTS), 0.01),
        'w_gate_up': nrm(ks[21], (DEPTH, N_EXPERTS, D_MODEL, 2 * D_EXPERT), D_MODEL ** -0.5),
        'b_gate_up': nrm(ks[22], (DEPTH, N_EXPERTS, 2 * D_EXPERT), 0.02),
        'w_down': nrm(ks[23], (DEPTH, N_EXPERTS, D_EXPERT, D_MODEL), D_EXPERT ** -0.5),
        'b_down': nrm(ks[24], (DEPTH, N_EXPERTS, D_MODEL), 0.02),
        'g_final': one(ks[25], (D_MODEL,)),
    }


def reference(x_prompt, x_sample, cache_k, cache_v, page_table, c_prompt, c_sample,
              rel_bias_table, w_ada, b_ada, g_mix, w_in, g_sgu, w_sp, b_sp,
              g_out_attn, g_out_sgu, w_out, g_ffn, w_router, b_router,
              w_gate_up, b_gate_up, w_down, b_down, g_final):
    xp, xs = x_prompt, x_sample
    last_chunk_start = ((x_prompt.shape[1] - 1) // SGU_CHUNK) * SGU_CHUNK
    k_p, v_p, k_s, v_s, sv_p, sv_s = [], [], [], [], [], []
    for l in range(DEPTH):
        params = (w_ada[l], b_ada[l], g_mix[l], w_in[l], g_sgu[l], w_sp[l], b_sp[l],
                  g_out_attn[l], g_out_sgu[l], w_out[l], g_ffn[l], w_router[l], b_router[l],
                  w_gate_up[l], b_gate_up[l], w_down[l], b_down[l])
        prompt_attn = functools.partial(moba_prompt, bias_table=rel_bias_table)
        sample_attn = functools.partial(moba_sample, cache_k=cache_k[l], cache_v=cache_v[l],
                                        page_table=page_table, bias_table=rel_bias_table)
        xp, kp, vp, svp = trunk_layer(xp, c_prompt, prompt_attn, *params)
        xs, kss, vss, svs = trunk_layer(xs, c_sample, sample_attn, *params)
        k_p.append(kp)
        v_p.append(vp)
        k_s.append(kss)
        v_s.append(vss)
        sv_p.append(svp[:, last_chunk_start:])
        sv_s.append(svs)
    y_prompt = rmsnorm(xp, g_final)
    y_sample = rmsnorm(xs, g_final)
    k_prompt = jnp.stack(k_p)
    v_prompt = jnp.stack(v_p)
    k_sample = jnp.stack(k_s)
    v_sample = jnp.stack(v_s)
    sgu_v_prompt = jnp.stack(sv_p)
    sgu_v_sample = jnp.stack(sv_s)
    return (y_prompt, y_sample, k_prompt, v_prompt, k_sample, v_sample, sgu_v_prompt, sgu_v_sample)
```

```python
import functools
import math

import numpy as np
import jax
import jax.numpy as jnp
from jax import lax
from jax.experimental import pallas as pl
from jax.experimental.pallas import tpu as pltpu

F32 = jnp.float32
BF16 = jnp.bfloat16
I32 = jnp.int32

D_MODEL = 2048
ATTN_WIDTH = 1024
SGU_WIDTH = 1024
HEAD_DIM = 128
N_HEADS = 8
N_GROUPS = 8
MOBA_BLOCK = 256
MOBA_TOPK = 3
SGU_CHUNK = 128
N_BUCKETS = 32
MAX_DISTANCE = 128
N_EXPERTS = 32
TOP_K = 4
SWIGLU_ALPHA = 1.702
SWIGLU_LIMIT = 7.0
RMS_EPS = 1e-5
NEG = -1e30
PICKED = -3e38
PAGE_SIZE = 128
LANES = 128
MASKED_BUCKET = N_BUCKETS

MOE_ROWS = 2048
MOE_CHUNK = 256
MOE_TILE = 256


def _cparams(sem, vmem_mb):
    return pltpu.CompilerParams(dimension_semantics=sem, vmem_limit_bytes=vmem_mb << 20)


def _rms(x, g):
    return x * lax.rsqrt(jnp.mean(x * x, axis=-1, keepdims=True) + RMS_EPS) * g


def _gelu(x):
    return 0.5 * x * (1.0 + lax.erf(x * (2.0 ** -0.5)))


def _adaln_kernel(c_ref, w_ref, b_ref, o_ref):
    c = c_ref[...]
    a = (c * jax.nn.sigmoid(c)).astype(BF16)
    o_ref[...] = jnp.dot(a, w_ref[...].astype(BF16), preferred_element_type=F32) + b_ref[...]


def _adaln(c, w_ada, b_ada):
    r, d = c.shape
    n = w_ada.shape[1]
    tn = 1024
    return pl.pallas_call(
        _adaln_kernel,
        grid=(n // tn,),
        in_specs=[pl.BlockSpec((r, d), lambda i: (0, 0)),
                  pl.BlockSpec((d, tn), lambda i: (0, i)),
                  pl.BlockSpec((1, tn), lambda i: (0, i))],
        out_specs=pl.BlockSpec((r, tn), lambda i: (0, i)),
        out_shape=jax.ShapeDtypeStruct((r, n), F32),
        compiler_params=_cparams(("arbitrary",), 40),
        name="adaln",
    )(c, w_ada, b_ada.reshape(1, n))


def _inproj_kernel(x_ref, sh_ref, sc_ref, gmix_ref, w_ref, gsgu_ref,
                   q_ref, k_ref, v_ref, u_ref, vs_ref, h_sc):
    n = pl.program_id(1)

    @pl.when(n == 0)
    def _():
        h = _rms(x_ref[...], gmix_ref[...]) * (1.0 + sc_ref[...]) + sh_ref[...]
        h_sc[...] = h.astype(BF16)

    p = jnp.dot(h_sc[...], w_ref[...], preferred_element_type=F32)

    @pl.when(n == 0)
    def _():
        q_ref[...] = p

    @pl.when(n == 1)
    def _():
        k_ref[...] = p

    @pl.when(n == 2)
    def _():
        v_ref[...] = p

    @pl.when(n == 3)
    def _():
        u_ref[...] = _gelu(p)

    @pl.when(n == 4)
    def _():
        g = _gelu(p)
        for i in range(N_GROUPS):
            sl = slice(i * LANES, (i + 1) * LANES)
            vs_ref[:, sl] = _rms(g[:, sl], gsgu_ref[:, sl])


def _inproj(x, mod3, g_mix, w_in_bf, g_sgu, tm):
    t, d = x.shape
    g_cnt, r, _ = mod3.shape
    tiles_per_group = (t // tm) // g_cnt
    w = ATTN_WIDTH
    mod_spec = lambda chunk: pl.BlockSpec((None, r, d), lambda i, n: (i // tiles_per_group, 0, chunk))
    out_spec = pl.BlockSpec((tm, w), lambda i, n: (i, 0))
    out_sds = jax.ShapeDtypeStruct((t, w), F32)
    return pl.pallas_call(
        _inproj_kernel,
        grid=(t // tm, 5),
        in_specs=[pl.BlockSpec((tm, d), lambda i, n: (i, 0)),
                  mod_spec(0), mod_spec(1),
                  pl.BlockSpec((1, d), lambda i, n: (0, 0)),
                  pl.BlockSpec((d, w), lambda i, n: (0, n)),
                  pl.BlockSpec((1, w), lambda i, n: (0, 0))],
        out_specs=[out_spec] * 5,
        out_shape=[out_sds] * 5,
        scratch_shapes=[pltpu.VMEM((tm, d), BF16)],
        compiler_params=_cparams(("parallel", "arbitrary"), 56),
        name="inproj",
    )(x, mod3, mod3, g_mix.reshape(1, d), w_in_bf, g_sgu.reshape(1, w))


def _bucket_np(dist):
    dist = np.maximum(dist, 0)
    max_exact = N_BUCKETS // 2
    dist_f = np.maximum(dist, max_exact).astype(np.float32)
    large = max_exact + (np.log(dist_f / np.float32(max_exact)) / np.float32(math.log(MAX_DISTANCE / max_exact))
                         * np.float32(N_BUCKETS - max_exact)).astype(np.int32)
    return np.where(dist < max_exact, dist, np.minimum(large, N_BUCKETS - 1)).astype(np.int32)


def _bias_prompt_kernel(tab_ref, bucket_ref, o_ref):
    h = pl.program_id(0)
    b = bucket_ref[...]
    acc = jnp.zeros(b.shape, F32)
    for kk in range(N_BUCKETS):
        acc = jnp.where(b == kk, tab_ref[kk, h], acc)
    o_ref[0] = jnp.where(b == MASKED_BUCKET, NEG, acc)


def _bias_prompt(table):
    r = np.arange(MOBA_BLOCK)[:, None]
    c = np.arange(MOBA_BLOCK)[None, :]
    own = np.where(c <= r, _bucket_np(r - c), MASKED_BUCKET)
    prev = _bucket_np(MOBA_BLOCK + r - c)
    far = _bucket_np(2 * MOBA_BLOCK + r - c)
    assert (far == far[0, 0]).all()
    buckets = jnp.asarray(np.stack([own, prev, far]).astype(np.int32))
    return pl.pallas_call(
        _bias_prompt_kernel,
        grid=(N_HEADS,),
        in_specs=[pl.BlockSpec(memory_space=pltpu.SMEM),
                  pl.BlockSpec((3, MOBA_BLOCK, MOBA_BLOCK), lambda h: (0, 0, 0))],
        out_specs=pl.BlockSpec((1, 3, MOBA_BLOCK, MOBA_BLOCK), lambda h: (h, 0, 0, 0)),
        out_shape=jax.ShapeDtypeStruct((N_HEADS, 3, MOBA_BLOCK, MOBA_BLOCK), F32),
        compiler_params=_cparams(("arbitrary",), 16),
        name="bias_prompt",
    )(table, buckets)


def _bias_sample_kernel(tabt_ref, bucket_ref, o_ref):
    b = bucket_ref[...]
    acc = jnp.zeros(b.shape, F32)
    for kk in range(N_BUCKETS):
        acc = jnp.where(b == kk, tabt_ref[:, kk:kk + 1], acc)
    o_ref[...] = jnp.where(b == MASKED_BUCKET, NEG, acc)


def _bias_sample(table, past_len, dec_seq):
    rows = dec_seq * N_HEADS
    qi = (np.arange(rows) // N_HEADS)[:, None]
    past = _bucket_np(past_len + qi - np.arange(past_len)[None, :])
    kj = np.arange(LANES)[None, :]
    own = np.where(kj <= qi, _bucket_np(qi - kj), MASKED_BUCKET)
    buckets = jnp.asarray(np.concatenate([past, own], axis=1).astype(np.int32))
    tab_t = jnp.tile(table, (1, dec_seq)).T
    return pl.pallas_call(
        _bias_sample_kernel,
        out_shape=jax.ShapeDtypeStruct(buckets.shape, F32),
        name="bias_sample",
    )(tab_t, buckets)


def _attn_prompt_kernel(q_ref, k_ref, v_ref, bias_ref, o_ref, kb_sc, vb_sc, s_sc):
    j = pl.program_id(2)
    blk = MOBA_BLOCK

    @pl.when(j == 0)
    def _():
        kb_sc[...] = k_ref[...].astype(BF16)
        vb_sc[...] = v_ref[...].astype(BF16)

    qb = q_ref[...].astype(BF16)
    scale = HEAD_DIM ** -0.5
    lane = lax.broadcasted_iota(I32, (blk, LANES), 1).astype(F32)
    nt = (((1,), (1,)), ((), ()))

    def rows(n):
        return pl.ds(pl.multiple_of(n * blk, blk), blk)

    def score(n, gates):
        s = lax.dot_general(qb, kb_sc[rows(n), :], nt, preferred_element_type=F32)
        s_sc[n] = s
        return jnp.where(lane == n.astype(F32), jnp.sum(s, axis=-1, keepdims=True), gates)

    gates = lax.fori_loop(0, j, score, jnp.full((blk, LANES), NEG, F32))

    allow = jnp.full((blk, LANES), NEG, F32)
    for _ in range(MOBA_TOPK):
        best = jnp.max(gates, axis=-1, keepdims=True)
        first = jnp.min(jnp.where(gates == best, lane, float(LANES)), axis=-1, keepdims=True)
        hit = lane == first
        allow = jnp.where(hit, 0.0, allow)
        gates = jnp.where(hit, PICKED, gates)

    s = lax.dot_general(qb, kb_sc[rows(j), :], nt, preferred_element_type=F32) * scale + bias_ref[0, 0]
    m = jnp.max(s, axis=-1, keepdims=True)
    p = jnp.exp(s - m)
    l = jnp.sum(p, axis=-1, keepdims=True)
    acc = jnp.dot(p.astype(BF16), vb_sc[rows(j), :], preferred_element_type=F32)

    def attend(n, carry):
        m, l, acc = carry
        a_n = jnp.sum(jnp.where(lane == n.astype(F32), allow, 0.0), axis=-1, keepdims=True)
        s = s_sc[n] * scale + bias_ref[0, jnp.minimum(j - n, 2)] + a_n
        m_new = jnp.maximum(m, jnp.max(s, axis=-1, keepdims=True))
        alpha = jnp.exp(m - m_new)
        p = jnp.exp(s - m_new)
        l = alpha * l + jnp.sum(p, axis=-1, keepdims=True)
        acc = alpha * acc + jnp.dot(p.astype(BF16), vb_sc[rows(n), :], preferred_element_type=F32)
        return m_new, l, acc

    m, l, acc = lax.fori_loop(0, j, attend, (m, l, acc))
    o_ref[...] = acc / l


def _attn_prompt(q, k, v, bias_tiles, batch, seq):
    nb = seq // MOBA_BLOCK
    qspec = pl.BlockSpec((MOBA_BLOCK, HEAD_DIM), lambda b, h, j: (b * nb + j, h))
    kvspec = pl.BlockSpec((seq, HEAD_DIM), lambda b, h, j: (b, h))
    return pl.pallas_call(
        _attn_prompt_kernel,
        grid=(batch, N_HEADS, nb),
        in_specs=[qspec, kvspec, kvspec,
                  pl.BlockSpec((1, 3, MOBA_BLOCK, MOBA_BLOCK), lambda b, h, j: (h, 0, 0, 0))],
        out_specs=qspec,
        out_shape=jax.ShapeDtypeStruct(q.shape, F32),
        scratch_shapes=[pltpu.VMEM((seq, HEAD_DIM), BF16), pltpu.VMEM((seq, HEAD_DIM), BF16),
                        pltpu.VMEM((nb, MOBA_BLOCK, MOBA_BLOCK), F32)],
        compiler_params=_cparams(("parallel", "parallel", "arbitrary"), 32),
        name="attn_prompt",
    )(q, k, v, bias_tiles)


PAGES_PER_STEP = 8


def _attn_sample_kernel(pt_ref, q_ref, kn_ref, vn_ref, bias_ref, *refs, n_pages, dec_seq):
    del pt_ref
    pps = PAGES_PER_STEP
    k_refs = refs[:pps]
    v_refs = refs[pps:2 * pps]
    o_ref, qe_sc, s_sc, acc_sc, den_sc = refs[2 * pps:]
    s = pl.program_id(1)
    k_steps = n_pages // pps
    rows = dec_seq * N_HEADS
    width = N_HEADS * HEAD_DIM
    n_blocks = n_pages * PAGE_SIZE // MOBA_BLOCK
    ppb = MOBA_BLOCK // PAGE_SIZE
    scale = HEAD_DIM ** -0.5
    sub = lax.broadcasted_iota(I32, (N_HEADS, width), 0)
    lane_head = lax.broadcasted_iota(I32, (N_HEADS, width), 1) // HEAD_DIM
    nt = (((1,), (1,)), ((), ()))

    @pl.when(s == 0)
    def _():
        q = q_ref[...]
        for i in range(dec_seq):
            qi = jnp.broadcast_to(q[i:i + 1, :], (N_HEADS, width))
            qe_sc[i * N_HEADS:(i + 1) * N_HEADS, :] = jnp.where(sub == lane_head, qi, 0.0)

    @pl.when(s < k_steps)
    def _():
        qe = qe_sc[...].astype(BF16)
        for pg in range(pps):
            kp = k_refs[pg][...].astype(BF16)
            s_sc[s * pps + pg] = lax.dot_general(qe, kp, nt, preferred_element_type=F32)

    @pl.when(s == k_steps - 1)
    def _():
        gates = []
        for n in range(n_blocks):
            tot = s_sc[n * ppb]
            for t in range(1, ppb):
                tot = tot + s_sc[n * ppb + t]
            gates.append(jnp.sum(tot, axis=-1, keepdims=True))
        allow = []
        for n in range(n_blocks):
            ahead = jnp.zeros((rows, 1), F32)
            for mm in range(n_blocks):
                if mm == n:
                    continue
                before = (gates[mm] >= gates[n]) if mm < n else (gates[mm] > gates[n])
                ahead = ahead + before.astype(F32)
            allow.append(jnp.where(ahead < MOBA_TOPK, 0.0, NEG))
        qe = qe_sc[...].astype(BF16).astype(F32)
        kn = kn_ref[...].astype(BF16).astype(F32)
        past_len = n_pages * PAGE_SIZE
        own = []
        for i in range(dec_seq):
            raw = jnp.sum(qe * kn[i:i + 1, :], axis=-1, keepdims=True)
            own.append(raw * scale + bias_ref[:, past_len + i:past_len + i + 1])
        m = own[0]
        for i in range(1, dec_seq):
            m = jnp.maximum(m, own[i])
        run = jnp.full((rows, PAGE_SIZE), NEG, F32)
        for t in range(n_pages):
            lg = s_sc[t] * scale + bias_ref[:, t * PAGE_SIZE:(t + 1) * PAGE_SIZE] + allow[t // ppb]
            s_sc[t] = lg
            run = jnp.maximum(run, lg)
        m = jnp.maximum(m, jnp.max(run, axis=-1, keepdims=True))
        den = jnp.zeros((rows, 1), F32)
        vn = vn_ref[...]
        acc = jnp.zeros((rows, width), F32)
        for i in range(dec_seq):
            p_own = jnp.exp(own[i] - m)
            den = den + p_own
            acc = acc + p_own * vn[i:i + 1, :]
        psum = jnp.zeros((rows, PAGE_SIZE), F32)
        for t in range(n_pages):
            p = jnp.exp(s_sc[t] - m)
            s_sc[t] = p
            psum = psum + p
        den_sc[...] = jnp.broadcast_to(den + jnp.sum(psum, axis=-1, keepdims=True), den_sc.shape)
        acc_sc[...] = acc

    @pl.when(s >= k_steps)
    def _():
        acc = acc_sc[...]
        for pg in range(pps):
            p = s_sc[(s - k_steps) * pps + pg].astype(BF16)
            acc = acc + jnp.dot(p, v_refs[pg][...].astype(BF16), preferred_element_type=F32)
        acc_sc[...] = acc

    @pl.when(s == 2 * k_steps - 1)
    def _():
        res = acc_sc[...] / den_sc[:, 0:1]
        for i in range(dec_seq):
            blk = jnp.where(sub == lane_head, res[i * N_HEADS:(i + 1) * N_HEADS, :], 0.0)
            o_ref[i:i + 1, :] = jnp.sum(blk, axis=0, keepdims=True)


def _attn_sample(q, k_new, v_new, cache_k, cache_v, page_table, bias_s):
    b, dec_seq, width = q.shape
    n_pages = page_table.shape[1]
    pps = PAGES_PER_STEP
    k_steps = n_pages // pps
    rows = dec_seq * N_HEADS
    tok_spec = pl.BlockSpec((None, dec_seq, width), lambda i, s, pt: (i, 0, 0))

    def k_spec(pg):
        return pl.BlockSpec((None, PAGE_SIZE, width),
                            lambda i, s, pt: (pt[i * n_pages + jnp.minimum(s, k_steps - 1) * pps + pg], 0, 0))

    def v_spec(pg):
        return pl.BlockSpec((None, PAGE_SIZE, width),
                            lambda i, s, pt: (pt[i * n_pages + jnp.maximum(s - k_steps, 0) * pps + pg], 0, 0))

    grid_spec = pltpu.PrefetchScalarGridSpec(
        num_scalar_prefetch=1,
        grid=(b, 2 * k_steps),
        in_specs=[tok_spec, tok_spec, tok_spec,
                  pl.BlockSpec(bias_s.shape, lambda i, s, pt: (0, 0))]
                 + [k_spec(pg) for pg in range(pps)] + [v_spec(pg) for pg in range(pps)],
        out_specs=tok_spec,
        scratch_shapes=[pltpu.VMEM((rows, width), F32),
                        pltpu.VMEM((n_pages, rows, PAGE_SIZE), F32),
                        pltpu.VMEM((rows, width), F32),
                        pltpu.VMEM((rows, LANES), F32)],
    )
    return pl.pallas_call(
        functools.partial(_attn_sample_kernel, n_pages=n_pages, dec_seq=dec_seq),
        grid_spec=grid_spec,
        out_shape=jax.ShapeDtypeStruct(q.shape, F32),
        compiler_params=_cparams(("parallel", "arbitrary"), 40),
        name="attn_sample",
    )(page_table.reshape(-1), q, k_new, v_new, bias_s, *([cache_k] * pps), *([cache_v] * pps))


def _mixout_kernel(attn_ref, u_ref, vs_ref, x_ref, gt_ref, sc_ref, sh_ref, wmix_ref, mask_ref, bmix_ref,
                   ga_ref, gs_ref, wout_ref, gffn_ref, wr_ref, br_ref,
                   x1_ref, h2_ref, topi_ref, topg_ref, sgu_sc):
    tm = x_ref.shape[0]
    for g in range(N_GROUPS):
        cols = slice(g * LANES, (g + 1) * LANES)
        wg = (wmix_ref[g] * mask_ref[...]).astype(BF16)
        for c in range(tm // SGU_CHUNK):
            rws = slice(c * SGU_CHUNK, (c + 1) * SGU_CHUNK)
            mixed = jnp.dot(wg, vs_ref[rws, cols].astype(BF16), preferred_element_type=F32) + bmix_ref[:, cols]
            sgu_sc[rws, cols] = u_ref[rws, cols] * mixed
    an = _rms(attn_ref[...], ga_ref[...]).astype(BF16)
    sn = _rms(sgu_sc[...], gs_ref[...]).astype(BF16)
    out = (jnp.dot(an, wout_ref[:ATTN_WIDTH, :], preferred_element_type=F32)
           + jnp.dot(sn, wout_ref[ATTN_WIDTH:, :], preferred_element_type=F32))
    x1 = x_ref[...] + gt_ref[...] * out
    x1_ref[...] = x1
    h2 = (_rms(x1, gffn_ref[...]) * (1.0 + sc_ref[...]) + sh_ref[...]).astype(BF16)
    h2_ref[...] = h2
    logits = jnp.dot(h2, wr_ref[...], preferred_element_type=F32) + br_ref[...]
    lane = lax.broadcasted_iota(I32, logits.shape, 1).astype(F32)
    idx_out = jnp.zeros(logits.shape, F32)
    val_out = jnp.zeros(logits.shape, F32)
    top = None
    den = None
    for r in range(TOP_K):
        best = jnp.max(logits, axis=-1, keepdims=True)
        first = jnp.min(jnp.where(logits == best, lane, float(LANES)), axis=-1, keepdims=True)
        if r == 0:
            top = best
        e = jnp.exp(best - top)
        den = e if r == 0 else den + e
        idx_out = jnp.where(lane == float(r), first, idx_out)
        val_out = jnp.where(lane == float(r), e, val_out)
        logits = jnp.where(lane == first, PICKED, logits)
    topi_ref[...] = idx_out.astype(I32)
    topg_ref[...] = val_out / den


def _mixout(attn, u, vs, x, mod3, wmix, mixmask, bmix, g_out_attn, g_out_sgu, w_out_bf, g_ffn, wr_pad, br_pad, tm):
    t, d = x.shape
    g_cnt, r, _ = mod3.shape
    tiles_per_group = (t // tm) // g_cnt
    w = ATTN_WIDTH
    mod_spec = lambda chunk: pl.BlockSpec((None, r, d), lambda i: (i // tiles_per_group, 0, chunk))
    half = pl.BlockSpec((tm, w), lambda i: (i, 0))
    full = pl.BlockSpec((tm, d), lambda i: (i, 0))
    small = pl.BlockSpec((tm, LANES), lambda i: (i, 0))
    const = lambda shape: pl.BlockSpec(shape, lambda i: (0,) * len(shape))
    return pl.pallas_call(
        _mixout_kernel,
        grid=(t // tm,),
        in_specs=[half, half, half, full, mod_spec(2), mod_spec(4), mod_spec(3),
                  const(wmix.shape), const(mixmask.shape), const(bmix.shape),
                  const((1, w)), const((1, w)), const((d, d)), const((1, d)),
                  const((d, LANES)), const((1, LANES))],
        out_specs=[full, full, small, small],
        out_shape=[jax.ShapeDtypeStruct((t, d), F32), jax.ShapeDtypeStruct((t, d), BF16),
                   jax.ShapeDtypeStruct((t, LANES), I32), jax.ShapeDtypeStruct((t, LANES), F32)],
        scratch_shapes=[pltpu.VMEM((tm, w), F32)],
        compiler_params=_cparams(("parallel",), 48),
        name="mixout",
    )(attn, u, vs, x, mod3, mod3, mod3, wmix, mixmask, bmix, g_out_attn.reshape(1, w), g_out_sgu.reshape(1, w),
      w_out_bf, g_ffn.reshape(1, d), wr_pad, br_pad)


def _moe_kernel(exp_ref, nv_ref, oblk_ref, x_ref, wg_ref, wu_ref, bg_ref, bu_ref, wd_ref, bd_ref, y_ref, act_sc):
    del exp_ref, oblk_ref
    sb = pl.program_id(0)
    t = pl.program_id(1)
    n_f = act_sc.shape[1] // MOE_TILE
    n_chunks = (nv_ref[sb] + MOE_CHUNK - 1) // MOE_CHUNK

    def rows(c):
        return pl.ds(pl.multiple_of(c * MOE_CHUNK, MOE_CHUNK), MOE_CHUNK)

    @pl.when(t < n_f)
    def _():
        wg = wg_ref[...].astype(BF16)
        wu = wu_ref[...].astype(BF16)
        bg = bg_ref[...]
        bu = bu_ref[...]

        def body(c, carry):
            xb = x_ref[rows(c), :]
            g = jnp.dot(xb, wg, preferred_element_type=F32) + bg
            up = jnp.dot(xb, wu, preferred_element_type=F32) + bu
            g = jnp.minimum(g, SWIGLU_LIMIT)
            up = jnp.clip(up, -SWIGLU_LIMIT, SWIGLU_LIMIT)
            act = g * jax.nn.sigmoid(SWIGLU_ALPHA * g) * (up + 1.0)
            act_sc[t, rows(c), :] = act.astype(BF16)
            return carry

        lax.fori_loop(0, n_chunks, body, 0)

    @pl.when(t >= n_f)
    def _():
        bd = bd_ref[...]

        def body(c, carry):
            acc = jnp.zeros((MOE_CHUNK, MOE_TILE), F32) + bd
            for f in range(n_f):
                wd = wd_ref[f * MOE_TILE:(f + 1) * MOE_TILE, :].astype(BF16)
                acc = acc + jnp.dot(act_sc[f, rows(c), :], wd, preferred_element_type=F32)
            y_ref[rows(c), :] = acc
            return carry

        lax.fori_loop(0, n_chunks, body, 0)


def _moe(xs, sb_expert, sb_rows, sb_out, w_gate_up, b_gate_up, w_down, b_down):
    p, d = xs.shape
    n_sb = p // MOE_ROWS
    f_dim = w_down.shape[1]
    n_f = f_dim // MOE_TILE
    n_o = d // MOE_TILE
    def f_tile(s, t, nv):
        return jnp.where(nv[s] > 0, jnp.minimum(t, n_f - 1), n_f - 1)

    def o_tile(s, t, nv):
        return jnp.where(nv[s] > 0, jnp.maximum(t - n_f, 0), n_o - 1)

    grid_spec = pltpu.PrefetchScalarGridSpec(
        num_scalar_prefetch=3,
        grid=(n_sb, n_f + n_o),
        in_specs=[
            pl.BlockSpec((MOE_ROWS, d), lambda s, t, ex, nv, ob: (jnp.minimum(s, ob[n_sb]), 0)),
            pl.BlockSpec((None, d, MOE_TILE), lambda s, t, ex, nv, ob: (ex[s], 0, f_tile(s, t, nv))),
            pl.BlockSpec((None, d, MOE_TILE), lambda s, t, ex, nv, ob: (ex[s], 0, n_f + f_tile(s, t, nv))),
            pl.BlockSpec((None, 1, MOE_TILE), lambda s, t, ex, nv, ob: (ex[s], 0, f_tile(s, t, nv))),
            pl.BlockSpec((None, 1, MOE_TILE), lambda s, t, ex, nv, ob: (ex[s], 0, n_f + f_tile(s, t, nv))),
            pl.BlockSpec((None, f_dim, MOE_TILE), lambda s, t, ex, nv, ob: (ex[s], 0, o_tile(s, t, nv))),
            pl.BlockSpec((None, 1, MOE_TILE), lambda s, t, ex, nv, ob: (ex[s], 0, o_tile(s, t, nv))),
        ],
        out_specs=pl.BlockSpec(
            (MOE_ROWS, MOE_TILE),
            lambda s, t, ex, nv, ob: (ob[s], jnp.where(nv[s] > 0, jnp.maximum(t - n_f, 0), 0))),
        scratch_shapes=[pltpu.VMEM((n_f, MOE_ROWS, MOE_TILE), BF16)],
    )
    e = w_gate_up.shape[0]
    return pl.pallas_call(
        _moe_kernel,
        grid_spec=grid_spec,
        out_shape=jax.ShapeDtypeStruct(((n_sb + 1) * MOE_ROWS, d), F32),
        compiler_params=_cparams(("arbitrary", "arbitrary"), 56),
        name="moe",
    )(sb_expert, sb_rows, sb_out, xs, w_gate_up, w_gate_up,
      b_gate_up.reshape(e, 1, -1), b_gate_up.reshape(e, 1, -1), w_down, b_down.reshape(e, 1, -1))


def _combine_kernel(x1_ref, gt_ref, ysel_ref, gate_ref, gfin_ref, o_ref):
    gate = gate_ref[...]
    y = jnp.zeros(x1_ref.shape, F32)
    d = x1_ref.shape[1]
    for kk in range(TOP_K):
        y = y + gate[:, kk:kk + 1] * ysel_ref[:, kk * d:(kk + 1) * d]
    x2 = x1_ref[...] + gt_ref[...] * y
    o_ref[...] = _rms(x2, gfin_ref[...])


def _combine(x1, mod3, ysel, gates, g_final, tm):
    t, d = x1.shape
    g_cnt, r, _ = mod3.shape
    tiles_per_group = (t // tm) // g_cnt
    full = pl.BlockSpec((tm, d), lambda i: (i, 0))
    return pl.pallas_call(
        _combine_kernel,
        grid=(t // tm,),
        in_specs=[full,
                  pl.BlockSpec((None, r, d), lambda i: (i // tiles_per_group, 0, 5)),
                  pl.BlockSpec((tm, TOP_K * d), lambda i: (i, 0)),
                  pl.BlockSpec((tm, LANES), lambda i: (i, 0)),
                  pl.BlockSpec((1, d), lambda i: (0, 0))],
        out_specs=full,
        out_shape=jax.ShapeDtypeStruct((t, d), F32),
        compiler_params=_cparams(("parallel",), 48),
        name="combine",
    )(x1, mod3, ysel, gates, g_final.reshape(1, d))


def _route(top_idx):
    n = top_idx.shape[0]
    n_assign = n * TOP_K
    n_sb = n_assign // MOE_ROWS + N_EXPERTS
    e_flat = top_idx.reshape(-1)
    onehot = (e_flat[:, None] == jnp.arange(N_EXPERTS)[None, :]).astype(I32)
    csum = jnp.cumsum(onehot, axis=0)
    counts = csum[-1]
    rank = jnp.take_along_axis(csum, e_flat[:, None], axis=1)[:, 0] - 1
    sb_per_e = (counts + MOE_ROWS - 1) // MOE_ROWS
    sb_end = jnp.cumsum(sb_per_e)
    sb_start = sb_end - sb_per_e
    n_used = sb_end[-1]
    dest = sb_start[e_flat] * MOE_ROWS + rank
    sb = jnp.arange(n_sb)
    sb_clamped = jnp.minimum(sb, n_used - 1)
    sb_expert = jnp.minimum(jnp.searchsorted(sb_end, sb_clamped, side='right'), N_EXPERTS - 1).astype(I32)
    local = sb_clamped - sb_start[sb_expert]
    sb_rows = jnp.where(sb < n_used, jnp.clip(counts[sb_expert] - local * MOE_ROWS, 0, MOE_ROWS), 0).astype(I32)
    sb_out = jnp.concatenate([jnp.where(sb < n_used, sb, n_sb), (n_used - 1)[None]]).astype(I32)
    src_tok = jnp.zeros((n_sb * MOE_ROWS,), I32).at[dest].set(jnp.arange(n_assign, dtype=I32) // TOP_K)
    return dest, src_tok, sb_expert, sb_rows, sb_out


def kernel(x_prompt, x_sample, cache_k, cache_v, page_table, c_prompt, c_sample, rel_bias_table, w_ada, b_ada,
           g_mix, w_in, g_sgu, w_sp, b_sp, g_out_attn, g_out_sgu, w_out, g_ffn, w_router, b_router,
           w_gate_up, b_gate_up, w_down, b_down, g_final):
    batch, seq, d = x_prompt.shape
    dec_batch, dec_seq, _ = x_sample.shape
    depth = w_ada.shape[0]
    assert depth == 1 and d == D_MODEL and seq % MOBA_BLOCK == 0
    n_p = batch * seq
    n_s = dec_batch * dec_seq
    past_len = page_table.shape[1] * PAGE_SIZE
    assert past_len % MOBA_BLOCK == 0 and dec_seq <= SGU_CHUNK and SGU_CHUNK % dec_seq == 0
    l = 0

    mod = _adaln(jnp.concatenate([c_prompt, c_sample], axis=0), w_ada[l], b_ada[l])
    mod_p = mod[:batch].reshape(batch, 1, 6 * d)
    tm_s = min(n_s, 256)
    assert n_s % tm_s == 0 and tm_s % SGU_CHUNK == 0
    mod_s = jnp.repeat(mod[batch:], dec_seq, axis=0).reshape(n_s // tm_s, tm_s, 6 * d)

    w_in_bf = w_in[l].astype(BF16)
    w_out_bf = w_out[l].astype(BF16)
    xp = x_prompt.reshape(n_p, d)
    xs = x_sample.reshape(n_s, d)
    g_sgu_flat = g_sgu[l].reshape(-1)

    q_p, k_p, v_p, u_p, vs_p = _inproj(xp, mod_p, g_mix[l], w_in_bf, g_sgu_flat, 512)
    q_s, k_s, v_s, u_s, vs_s = _inproj(xs, mod_s, g_mix[l], w_in_bf, g_sgu_flat, tm_s)

    attn_p = _attn_prompt(q_p, k_p, v_p, _bias_prompt(rel_bias_table), batch, seq)
    width = N_HEADS * HEAD_DIM
    attn_s = _attn_sample(q_s.reshape(dec_batch, dec_seq, width), k_s.reshape(dec_batch, dec_seq, width),
                          v_s.reshape(dec_batch, dec_seq, width),
                          cache_k[l].reshape(-1, PAGE_SIZE, width), cache_v[l].reshape(-1, PAGE_SIZE, width),
                          page_table, _bias_sample(rel_bias_table, past_len, dec_seq)).reshape(n_s, width)

    tri = jnp.tril(jnp.ones((SGU_CHUNK, SGU_CHUNK), F32))
    reps = SGU_CHUNK // dec_seq
    seq_id = jnp.arange(SGU_CHUNK) // dec_seq
    mask_s = tri * (seq_id[:, None] == seq_id[None, :]).astype(F32)
    wmix_s = jnp.tile(w_sp[l][:, :dec_seq, :dec_seq], (1, reps, reps))
    bmix_p = jnp.repeat(b_sp[l].T, SGU_WIDTH // N_GROUPS, axis=1)
    bmix_s = jnp.repeat(jnp.tile(b_sp[l][:, :dec_seq], (1, reps)).T, SGU_WIDTH // N_GROUPS, axis=1)
    wr_pad = jnp.pad(w_router[l], ((0, 0), (0, LANES - N_EXPERTS))).astype(BF16)
    br_pad = jnp.pad(b_router[l], (0, LANES - N_EXPERTS), constant_values=NEG).reshape(1, LANES)

    x1_p, h2_p, ti_p, tg_p = _mixout(attn_p, u_p, vs_p, xp, mod_p, w_sp[l], tri, bmix_p, g_out_attn[l],
                                     g_out_sgu[l], w_out_bf, g_ffn[l], wr_pad, br_pad, 256)
    x1_s, h2_s, ti_s, tg_s = _mixout(attn_s, u_s, vs_s, xs, mod_s, wmix_s, mask_s, bmix_s, g_out_attn[l],
                                     g_out_sgu[l], w_out_bf, g_ffn[l], wr_pad, br_pad, tm_s)

    h2 = jnp.concatenate([h2_p, h2_s], axis=0)
    top_idx = jnp.concatenate([ti_p, ti_s], axis=0)[:, :TOP_K]
    dest, src_tok, sb_expert, sb_rows, sb_out = _route(top_idx)
    xs_sorted = jnp.take(h2, src_tok, axis=0)
    y_rows = _moe(xs_sorted, sb_expert, sb_rows, sb_out, w_gate_up[l], b_gate_up[l], w_down[l], b_down[l])
    ysel = jnp.take(y_rows, dest, axis=0).reshape(n_p + n_s, TOP_K * d)

    y_p = _combine(x1_p, mod_p, ysel[:n_p], tg_p, g_final, 256)
    y_s = _combine(x1_s, mod_s, ysel[n_p:], tg_s, g_final, tm_s)

    hd = (N_HEADS, HEAD_DIM)
    gd = (N_GROUPS, SGU_WIDTH // N_GROUPS)
    last_chunk = ((seq - 1) // SGU_CHUNK) * SGU_CHUNK
    return (y_p.reshape(batch, seq, d), y_s.reshape(dec_batch, dec_seq, d),
            k_p.reshape(1, batch, seq, *hd), v_p.reshape(1, batch, seq, *hd),
            k_s.reshape(1, dec_batch, dec_seq, *hd), v_s.reshape(1, dec_batch, dec_seq, *hd),
            vs_p.reshape(batch, seq, *gd)[:, last_chunk:][None], vs_s.reshape(1, dec_batch, dec_seq, *gd))
```

```python
import functools
import math

import numpy as np
import jax
import jax.numpy as jnp
from jax import lax
from jax.experimental import pallas as pl
from jax.experimental.pallas import tpu as pltpu

F32 = jnp.float32
BF16 = jnp.bfloat16
I32 = jnp.int32

D_MODEL = 2048
ATTN_WIDTH = 1024
SGU_WIDTH = 1024
HEAD_DIM = 128
N_HEADS = 8
N_GROUPS = 8
MOBA_BLOCK = 256
MOBA_TOPK = 3
SGU_CHUNK = 128
N_BUCKETS = 32
MAX_DISTANCE = 128
N_EXPERTS = 32
TOP_K = 4
SWIGLU_ALPHA = 1.702
SWIGLU_LIMIT = 7.0
RMS_EPS = 1e-5
NEG = -1e30
PICKED = -3e38
PAGE_SIZE = 128
LANES = 128
MASKED_BUCKET = N_BUCKETS

MOE_ROWS = 3072
MOE_CHUNK = 256
MOE_TILE = 256
GATHER_ROWS = 128
BF16_SUBLANES = 16


def _cparams(sem, vmem_mb):
    return pltpu.CompilerParams(dimension_semantics=sem, vmem_limit_bytes=vmem_mb << 20)


def _rms(x, g):
    return x * lax.rsqrt(jnp.mean(x * x, axis=-1, keepdims=True) + RMS_EPS) * g


def _gelu(x):
    return 0.5 * x * (1.0 + lax.erf(x * (2.0 ** -0.5)))


def _adaln_kernel(c_ref, w_ref, b_ref, o_ref):
    c = c_ref[...]
    a = (c * jax.nn.sigmoid(c)).astype(BF16)
    o_ref[...] = jnp.dot(a, w_ref[...].astype(BF16), preferred_element_type=F32) + b_ref[...]


def _adaln(c, w_ada, b_ada):
    r, d = c.shape
    n = w_ada.shape[1]
    tn = 1024
    return pl.pallas_call(
        _adaln_kernel,
        grid=(n // tn,),
        in_specs=[pl.BlockSpec((r, d), lambda i: (0, 0)),
                  pl.BlockSpec((d, tn), lambda i: (0, i)),
                  pl.BlockSpec((1, tn), lambda i: (0, i))],
        out_specs=pl.BlockSpec((r, tn), lambda i: (0, i)),
        out_shape=jax.ShapeDtypeStruct((r, n), F32),
        compiler_params=_cparams(("arbitrary",), 40),
        name="adaln",
    )(c, w_ada, b_ada.reshape(1, n))


def _inproj_kernel(x_ref, sh_ref, sc_ref, gmix_ref, w_ref, gsgu_ref,
                   q_ref, k_ref, v_ref, u_ref, vs_ref, h_sc):
    n = pl.program_id(1)

    @pl.when(n == 0)
    def _():
        h = _rms(x_ref[...], gmix_ref[...]) * (1.0 + sc_ref[...]) + sh_ref[...]
        h_sc[...] = h.astype(BF16)

    p = jnp.dot(h_sc[...], w_ref[...], preferred_element_type=F32)

    @pl.when(n == 0)
    def _():
        q_ref[...] = p

    @pl.when(n == 1)
    def _():
        k_ref[...] = p

    @pl.when(n == 2)
    def _():
        v_ref[...] = p

    @pl.when(n == 3)
    def _():
        u_ref[...] = _gelu(p)

    @pl.when(n == 4)
    def _():
        g = _gelu(p)
        for i in range(N_GROUPS):
            sl = slice(i * LANES, (i + 1) * LANES)
            vs_ref[:, sl] = _rms(g[:, sl], gsgu_ref[:, sl])


def _inproj(x, mod3, g_mix, w_in_bf, g_sgu, tm):
    t, d = x.shape
    g_cnt, r, _ = mod3.shape
    tiles_per_group = (t // tm) // g_cnt
    w = ATTN_WIDTH
    mod_spec = lambda chunk: pl.BlockSpec((None, r, d), lambda i, n: (i // tiles_per_group, 0, chunk))
    out_spec = pl.BlockSpec((tm, w), lambda i, n: (i, 0))
    out_sds = jax.ShapeDtypeStruct((t, w), F32)
    return pl.pallas_call(
        _inproj_kernel,
        grid=(t // tm, 5),
        in_specs=[pl.BlockSpec((tm, d), lambda i, n: (i, 0)),
                  mod_spec(0), mod_spec(1),
                  pl.BlockSpec((1, d), lambda i, n: (0, 0)),
                  pl.BlockSpec((d, w), lambda i, n: (0, n)),
                  pl.BlockSpec((1, w), lambda i, n: (0, 0))],
        out_specs=[out_spec] * 5,
        out_shape=[out_sds] * 5,
        scratch_shapes=[pltpu.VMEM((tm, d), BF16)],
        compiler_params=_cparams(("parallel", "arbitrary"), 56),
        name="inproj",
    )(x, mod3, mod3, g_mix.reshape(1, d), w_in_bf, g_sgu.reshape(1, w))


def _bucket_np(dist):
    dist = np.maximum(dist, 0)
    max_exact = N_BUCKETS // 2
    dist_f = np.maximum(dist, max_exact).astype(np.float32)
    large = max_exact + (np.log(dist_f / np.float32(max_exact)) / np.float32(math.log(MAX_DISTANCE / max_exact))
                         * np.float32(N_BUCKETS - max_exact)).astype(np.int32)
    return np.where(dist < max_exact, dist, np.minimum(large, N_BUCKETS - 1)).astype(np.int32)


def _bias_prompt_kernel(tab_ref, bucket_ref, o_ref):
    h = pl.program_id(0)
    b = bucket_ref[...]
    acc = jnp.zeros(b.shape, F32)
    for kk in range(N_BUCKETS):
        acc = jnp.where(b == kk, tab_ref[kk, h], acc)
    o_ref[0] = jnp.where(b == MASKED_BUCKET, NEG, acc)


def _bias_prompt(table):
    r = np.arange(MOBA_BLOCK)[:, None]
    c = np.arange(MOBA_BLOCK)[None, :]
    own = np.where(c <= r, _bucket_np(r - c), MASKED_BUCKET)
    prev = _bucket_np(MOBA_BLOCK + r - c)
    far = _bucket_np(2 * MOBA_BLOCK + r - c)
    assert (far == far[0, 0]).all()
    buckets = jnp.asarray(np.stack([own, prev, far]).astype(np.int32))
    return pl.pallas_call(
        _bias_prompt_kernel,
        grid=(N_HEADS,),
        in_specs=[pl.BlockSpec(memory_space=pltpu.SMEM),
                  pl.BlockSpec((3, MOBA_BLOCK, MOBA_BLOCK), lambda h: (0, 0, 0))],
        out_specs=pl.BlockSpec((1, 3, MOBA_BLOCK, MOBA_BLOCK), lambda h: (h, 0, 0, 0)),
        out_shape=jax.ShapeDtypeStruct((N_HEADS, 3, MOBA_BLOCK, MOBA_BLOCK), F32),
        compiler_params=_cparams(("arbitrary",), 16),
        name="bias_prompt",
    )(table, buckets)


def _bias_sample_kernel(tabt_ref, bucket_ref, o_ref):
    b = bucket_ref[...]
    acc = jnp.zeros(b.shape, F32)
    for kk in range(N_BUCKETS):
        acc = jnp.where(b == kk, tabt_ref[:, kk:kk + 1], acc)
    o_ref[...] = jnp.where(b == MASKED_BUCKET, NEG, acc)


def _bias_sample(table, past_len, dec_seq):
    rows = dec_seq * N_HEADS
    qi = (np.arange(rows) // N_HEADS)[:, None]
    past = _bucket_np(past_len + qi - np.arange(past_len)[None, :])
    kj = np.arange(LANES)[None, :]
    own = np.where(kj <= qi, _bucket_np(qi - kj), MASKED_BUCKET)
    buckets = jnp.asarray(np.concatenate([past, own], axis=1).astype(np.int32))
    tab_t = jnp.tile(table, (1, dec_seq)).T
    return pl.pallas_call(
        _bias_sample_kernel,
        out_shape=jax.ShapeDtypeStruct(buckets.shape, F32),
        name="bias_sample",
    )(tab_t, buckets)


def _attn_prompt_kernel(q_ref, k_ref, v_ref, bias_ref, o_ref, ka_sc, vb_sc, km_sc):
    j = pl.program_id(2)
    blk = MOBA_BLOCK
    seq = k_ref.shape[0]
    nb = seq // blk
    mrows = km_sc.shape[0]

    @pl.when(j == 0)
    def _():
        k = k_ref[...]
        ka_sc[:, :HEAD_DIM] = k.astype(BF16)
        row_blk = lax.broadcasted_iota(I32, (seq, LANES), 0) // blk
        ka_sc[:, HEAD_DIM:] = (lax.broadcasted_iota(I32, (seq, LANES), 1) == row_blk).astype(BF16)
        vb_sc[...] = v_ref[...].astype(BF16)
        means = [jnp.mean(k[n * blk:(n + 1) * blk, :], axis=0, keepdims=True) for n in range(nb)]
        means.append(jnp.zeros((mrows - nb, HEAD_DIM), F32))
        km_sc[...] = jnp.concatenate(means, axis=0).astype(BF16)

    qb = q_ref[...].astype(BF16)
    scale = HEAD_DIM ** -0.5
    nt = (((1,), (1,)), ((), ()))

    def rows(n):
        return pl.ds(pl.multiple_of(n * blk, blk), blk)

    sub = lax.broadcasted_iota(I32, (mrows, blk), 0).astype(F32)
    gates = lax.dot_general(km_sc[...], qb, nt, preferred_element_type=F32)
    gates = jnp.where(sub < j.astype(F32), gates, NEG)
    allow = jnp.full((mrows, blk), NEG, F32)
    for _ in range(MOBA_TOPK):
        best = jnp.max(gates, axis=0, keepdims=True)
        first = jnp.min(jnp.where(gates == best, sub, float(mrows)), axis=0, keepdims=True)
        hit = sub == first
        allow = jnp.where(hit, 0.0, allow)
        gates = jnp.where(hit, PICKED, gates)
    allow = jnp.concatenate([allow, jnp.full((LANES - mrows, blk), NEG, F32)], axis=0).T
    q_ext = jnp.concatenate([qb, allow.astype(BF16)], axis=1)

    s = lax.dot_general(qb, ka_sc[rows(j), :HEAD_DIM], nt, preferred_element_type=F32) * scale + bias_ref[0, 0]
    m = jnp.max(s, axis=-1, keepdims=True)
    p = jnp.exp(s - m)
    l = jnp.sum(p, axis=-1, keepdims=True)
    acc = jnp.dot(p.astype(BF16), vb_sc[rows(j), :], preferred_element_type=F32)

    def attend(n, carry):
        m, l, acc = carry
        s = lax.dot_general(q_ext, ka_sc[rows(n), :], nt, preferred_element_type=F32) * scale
        s = s + bias_ref[0, jnp.minimum(j - n, 2)]
        m_new = jnp.maximum(m, jnp.max(s, axis=-1, keepdims=True))
        alpha = jnp.exp(m - m_new)
        p = jnp.exp(s - m_new)
        l = alpha * l + jnp.sum(p, axis=-1, keepdims=True)
        acc = alpha * acc + jnp.dot(p.astype(BF16), vb_sc[rows(n), :], preferred_element_type=F32)
        return m_new, l, acc

    m, l, acc = lax.fori_loop(0, j, attend, (m, l, acc))
    o_ref[...] = acc / l


def _attn_prompt(q, k, v, bias_tiles, batch, seq):
    nb = seq // MOBA_BLOCK
    qspec = pl.BlockSpec((MOBA_BLOCK, HEAD_DIM), lambda b, h, j: (b * nb + j, h))
    kvspec = pl.BlockSpec((seq, HEAD_DIM), lambda b, h, j: (b, h))
    return pl.pallas_call(
        _attn_prompt_kernel,
        grid=(batch, N_HEADS, nb),
        in_specs=[qspec, kvspec, kvspec,
                  pl.BlockSpec((1, 3, MOBA_BLOCK, MOBA_BLOCK), lambda b, h, j: (h, 0, 0, 0))],
        out_specs=qspec,
        out_shape=jax.ShapeDtypeStruct(q.shape, F32),
        scratch_shapes=[pltpu.VMEM((seq, HEAD_DIM + LANES), BF16), pltpu.VMEM((seq, HEAD_DIM), BF16),
                        pltpu.VMEM((pl.cdiv(nb, BF16_SUBLANES) * BF16_SUBLANES, HEAD_DIM), BF16)],
        compiler_params=_cparams(("parallel", "parallel", "arbitrary"), 32),
        name="attn_prompt",
    )(q, k, v, bias_tiles)


PAGES_PER_STEP = 8


def _attn_sample_kernel(pt_ref, q_ref, kn_ref, vn_ref, bias_ref, *refs, n_pages, dec_seq):
    del pt_ref
    pps = PAGES_PER_STEP
    k_refs = refs[:pps]
    v_refs = refs[pps:2 * pps]
    o_ref, qe_sc, s_sc, acc_sc, den_sc = refs[2 * pps:]
    s = pl.program_id(1)
    k_steps = n_pages // pps
    rows = dec_seq * N_HEADS
    width = N_HEADS * HEAD_DIM
    n_blocks = n_pages * PAGE_SIZE // MOBA_BLOCK
    ppb = MOBA_BLOCK // PAGE_SIZE
    scale = HEAD_DIM ** -0.5
    sub = lax.broadcasted_iota(I32, (N_HEADS, width), 0)
    lane_head = lax.broadcasted_iota(I32, (N_HEADS, width), 1) // HEAD_DIM
    nt = (((1,), (1,)), ((), ()))

    @pl.when(s == 0)
    def _():
        q = q_ref[...]
        for i in range(dec_seq):
            qi = jnp.broadcast_to(q[i:i + 1, :], (N_HEADS, width))
            qe_sc[i * N_HEADS:(i + 1) * N_HEADS, :] = jnp.where(sub == lane_head, qi, 0.0)

    @pl.when(s < k_steps)
    def _():
        qe = qe_sc[...].astype(BF16)
        for pg in range(pps):
            kp = k_refs[pg][...].astype(BF16)
            s_sc[s * pps + pg] = lax.dot_general(qe, kp, nt, preferred_element_type=F32)

    @pl.when(s == k_steps - 1)
    def _():
        gates = []
        for n in range(n_blocks):
            tot = s_sc[n * ppb]
            for t in range(1, ppb):
                tot = tot + s_sc[n * ppb + t]
            gates.append(jnp.sum(tot, axis=-1, keepdims=True))
        allow = []
        for n in range(n_blocks):
            ahead = jnp.zeros((rows, 1), F32)
            for mm in range(n_blocks):
                if mm == n:
                    continue
                before = (gates[mm] >= gates[n]) if mm < n else (gates[mm] > gates[n])
                ahead = ahead + before.astype(F32)
            allow.append(jnp.where(ahead < MOBA_TOPK, 0.0, NEG))
        qe = qe_sc[...].astype(BF16).astype(F32)
        kn = kn_ref[...].astype(BF16).astype(F32)
        past_len = n_pages * PAGE_SIZE
        own = []
        for i in range(dec_seq):
            raw = jnp.sum(qe * kn[i:i + 1, :], axis=-1, keepdims=True)
            own.append(raw * scale + bias_ref[:, past_len + i:past_len + i + 1])
        m = own[0]
        for i in range(1, dec_seq):
            m = jnp.maximum(m, own[i])
        run = jnp.full((rows, PAGE_SIZE), NEG, F32)
        for t in range(n_pages):
            lg = s_sc[t] * scale + bias_ref[:, t * PAGE_SIZE:(t + 1) * PAGE_SIZE] + allow[t // ppb]
            s_sc[t] = lg
            run = jnp.maximum(run, lg)
        m = jnp.maximum(m, jnp.max(run, axis=-1, keepdims=True))
        den = jnp.zeros((rows, 1), F32)
        vn = vn_ref[...]
        acc = jnp.zeros((rows, width), F32)
        for i in range(dec_seq):
            p_own = jnp.exp(own[i] - m)
            den = den + p_own
            acc = acc + p_own * vn[i:i + 1, :]
        psum = jnp.zeros((rows, PAGE_SIZE), F32)
        for t in range(n_pages):
            p = jnp.exp(s_sc[t] - m)
            s_sc[t] = p
            psum = psum + p
        den_sc[...] = jnp.broadcast_to(den + jnp.sum(psum, axis=-1, keepdims=True), den_sc.shape)
        acc_sc[...] = acc

    @pl.when(s >= k_steps)
    def _():
        acc = acc_sc[...]
        for pg in range(pps):
            p = s_sc[(s - k_steps) * pps + pg].astype(BF16)
            acc = acc + jnp.dot(p, v_refs[pg][...].astype(BF16), preferred_element_type=F32)
        acc_sc[...] = acc

    @pl.when(s == 2 * k_steps - 1)
    def _():
        res = acc_sc[...] / den_sc[:, 0:1]
        for i in range(dec_seq):
            blk = jnp.where(sub == lane_head, res[i * N_HEADS:(i + 1) * N_HEADS, :], 0.0)
            o_ref[i:i + 1, :] = jnp.sum(blk, axis=0, keepdims=True)


def _attn_sample(q, k_new, v_new, cache_k, cache_v, page_table, bias_s):
    b, dec_seq, width = q.shape
    n_pages = page_table.shape[1]
    pps = PAGES_PER_STEP
    k_steps = n_pages // pps
    rows = dec_seq * N_HEADS
    tok_spec = pl.BlockSpec((None, dec_seq, width), lambda i, s, pt: (i, 0, 0))

    def k_spec(pg):
        return pl.BlockSpec((None, PAGE_SIZE, width),
                            lambda i, s, pt: (pt[i * n_pages + jnp.minimum(s, k_steps - 1) * pps + pg], 0, 0))

    def v_spec(pg):
        return pl.BlockSpec((None, PAGE_SIZE, width),
                            lambda i, s, pt: (pt[i * n_pages + jnp.maximum(s - k_steps, 0) * pps + pg], 0, 0))

    grid_spec = pltpu.PrefetchScalarGridSpec(
        num_scalar_prefetch=1,
        grid=(b, 2 * k_steps),
        in_specs=[tok_spec, tok_spec, tok_spec,
                  pl.BlockSpec(bias_s.shape, lambda i, s, pt: (0, 0))]
                 + [k_spec(pg) for pg in range(pps)] + [v_spec(pg) for pg in range(pps)],
        out_specs=tok_spec,
        scratch_shapes=[pltpu.VMEM((rows, width), F32),
                        pltpu.VMEM((n_pages, rows, PAGE_SIZE), F32),
                        pltpu.VMEM((rows, width), F32),
                        pltpu.VMEM((rows, LANES), F32)],
    )
    return pl.pallas_call(
        functools.partial(_attn_sample_kernel, n_pages=n_pages, dec_seq=dec_seq),
        grid_spec=grid_spec,
        out_shape=jax.ShapeDtypeStruct(q.shape, F32),
        compiler_params=_cparams(("parallel", "arbitrary"), 40),
        name="attn_sample",
    )(page_table.reshape(-1), q, k_new, v_new, bias_s, *([cache_k] * pps), *([cache_v] * pps))


def _mixout_kernel(attn_ref, u_ref, vs_ref, x_ref, gt_ref, sc_ref, sh_ref, wmix_ref, mask_ref, bmix_ref,
                   ga_ref, gs_ref, wout_ref, gffn_ref, wr_ref, br_ref,
                   x1_ref, h2_ref, topi_ref, topg_ref, sgu_sc):
    tm = x_ref.shape[0]
    for g in range(N_GROUPS):
        cols = slice(g * LANES, (g + 1) * LANES)
        wg = (wmix_ref[g] * mask_ref[...]).astype(BF16)
        for c in range(tm // SGU_CHUNK):
            rws = slice(c * SGU_CHUNK, (c + 1) * SGU_CHUNK)
            mixed = jnp.dot(wg, vs_ref[rws, cols].astype(BF16), preferred_element_type=F32) + bmix_ref[:, cols]
            sgu_sc[rws, cols] = u_ref[rws, cols] * mixed
    an = _rms(attn_ref[...], ga_ref[...]).astype(BF16)
    sn = _rms(sgu_sc[...], gs_ref[...]).astype(BF16)
    out = (jnp.dot(an, wout_ref[:ATTN_WIDTH, :], preferred_element_type=F32)
           + jnp.dot(sn, wout_ref[ATTN_WIDTH:, :], preferred_element_type=F32))
    x1 = x_ref[...] + gt_ref[...] * out
    x1_ref[...] = x1
    h2 = _rms(x1, gffn_ref[...]) * (1.0 + sc_ref[...]) + sh_ref[...]
    h2_ref[...] = h2
    logits = jnp.dot(h2.astype(BF16), wr_ref[...], preferred_element_type=F32) + br_ref[...]
    lane = lax.broadcasted_iota(I32, logits.shape, 1).astype(F32)
    idx_out = jnp.zeros(logits.shape, F32)
    val_out = jnp.zeros(logits.shape, F32)
    top = None
    den = None
    for r in range(TOP_K):
        best = jnp.max(logits, axis=-1, keepdims=True)
        first = jnp.min(jnp.where(logits == best, lane, float(LANES)), axis=-1, keepdims=True)
        if r == 0:
            top = best
        e = jnp.exp(best - top)
        den = e if r == 0 else den + e
        idx_out = jnp.where(lane == float(r), first, idx_out)
        val_out = jnp.where(lane == float(r), e, val_out)
        logits = jnp.where(lane == first, PICKED, logits)
    topi_ref[...] = idx_out.astype(I32)
    topg_ref[...] = val_out / den


def _mixout(attn, u, vs, x, mod3, wmix, mixmask, bmix, g_out_attn, g_out_sgu, w_out_bf, g_ffn, wr_pad, br_pad, tm):
    t, d = x.shape
    g_cnt, r, _ = mod3.shape
    tiles_per_group = (t // tm) // g_cnt
    w = ATTN_WIDTH
    mod_spec = lambda chunk: pl.BlockSpec((None, r, d), lambda i: (i // tiles_per_group, 0, chunk))
    half = pl.BlockSpec((tm, w), lambda i: (i, 0))
    full = pl.BlockSpec((tm, d), lambda i: (i, 0))
    small = pl.BlockSpec((tm, LANES), lambda i: (i, 0))
    const = lambda shape: pl.BlockSpec(shape, lambda i: (0,) * len(shape))
    return pl.pallas_call(
        _mixout_kernel,
        grid=(t // tm,),
        in_specs=[half, half, half, full, mod_spec(2), mod_spec(4), mod_spec(3),
                  const(wmix.shape), const(mixmask.shape), const(bmix.shape),
                  const((1, w)), const((1, w)), const((d, d)), const((1, d)),
                  const((d, LANES)), const((1, LANES))],
        out_specs=[full, full, small, small],
        out_shape=[jax.ShapeDtypeStruct((t, d), F32), jax.ShapeDtypeStruct((t, d), F32),
                   jax.ShapeDtypeStruct((t, LANES), I32), jax.ShapeDtypeStruct((t, LANES), F32)],
        scratch_shapes=[pltpu.VMEM((tm, w), F32)],
        compiler_params=_cparams(("parallel",), 48),
        name="mixout",
    )(attn, u, vs, x, mod3, mod3, mod3, wmix, mixmask, bmix, g_out_attn.reshape(1, w), g_out_sgu.reshape(1, w),
      w_out_bf, g_ffn.reshape(1, d), wr_pad, br_pad)


def _moe_kernel(src_ref, exp_ref, nv_ref, off_ref, oblk_ref, h_hbm, wg_ref, wu_ref, bg_ref, bu_ref, wd_ref, bd_ref,
                y_ref, x_sc, act_sc, stage_sc, sem):
    del exp_ref, oblk_ref
    sb = pl.program_id(0)
    t = pl.program_id(1)
    n_f = act_sc.shape[0]
    nv = nv_ref[sb]
    n_chunks = (nv + MOE_CHUNK - 1) // MOE_CHUNK

    @pl.when((t == 0) & (nv > 0))
    def _():
        off = off_ref[sb]
        last = off + nv - 1
        n_batches = n_chunks * (MOE_CHUNK // GATHER_ROWS)

        def row_copy(bt, slot, r):
            tok = src_ref[jnp.minimum(off + bt * GATHER_ROWS + r, last)]
            return pltpu.make_async_copy(h_hbm.at[pl.ds(tok, 1)], stage_sc.at[slot, pl.ds(r, 1)], sem.at[slot])

        def issue(bt, slot):
            def row(r, c):
                row_copy(bt, slot, r).start()
                return c
            lax.fori_loop(0, GATHER_ROWS, row, 0)

        def drain(bt, slot):
            def row(r, c):
                row_copy(bt, slot, r).wait()
                return c
            lax.fori_loop(0, GATHER_ROWS, row, 0)
            x_sc[pl.ds(pl.multiple_of(bt * GATHER_ROWS, GATHER_ROWS), GATHER_ROWS), :] = stage_sc[slot].astype(BF16)

        issue(0, 0)

        def body(bt, c):
            slot = bt % 2

            @pl.when(bt + 1 < n_batches)
            def _():
                issue(bt + 1, 1 - slot)

            drain(bt, slot)
            return c

        lax.fori_loop(0, n_batches, body, 0)

    def cascade(fn):
        big = 4 * MOE_CHUNK
        n_big = n_chunks // 4

        def body(i, c):
            fn(pl.multiple_of(i * big, big), big)
            return c

        lax.fori_loop(0, n_big, body, 0)
        rem = n_chunks - n_big * 4
        base = n_big * big

        @pl.when(rem >= 2)
        def _():
            fn(pl.multiple_of(base, MOE_CHUNK), 2 * MOE_CHUNK)

        @pl.when(rem % 2 == 1)
        def _():
            fn(pl.multiple_of(base + (rem // 2) * (2 * MOE_CHUNK), MOE_CHUNK), MOE_CHUNK)

    @pl.when(t < n_f)
    def _():
        wg = wg_ref[...].astype(BF16)
        wu = wu_ref[...].astype(BF16)
        bg = bg_ref[...]
        bu = bu_ref[...]

        def up_proj(start, size):
            xb = x_sc[pl.ds(start, size), :]
            g = jnp.dot(xb, wg, preferred_element_type=F32) + bg
            up = jnp.dot(xb, wu, preferred_element_type=F32) + bu
            g = jnp.minimum(g, SWIGLU_LIMIT)
            up = jnp.clip(up, -SWIGLU_LIMIT, SWIGLU_LIMIT)
            act = g * jax.nn.sigmoid(SWIGLU_ALPHA * g) * (up + 1.0)
            act_sc[t, pl.ds(start, size), :] = act.astype(BF16)

        cascade(up_proj)

    @pl.when(t >= n_f)
    def _():
        wd = wd_ref[...].astype(BF16)
        bd = bd_ref[...]

        def down_proj(start, size):
            acc = jnp.zeros((size, MOE_TILE), F32) + bd
            for f in range(n_f):
                acc = acc + jnp.dot(act_sc[f, pl.ds(start, size), :], wd[f * MOE_TILE:(f + 1) * MOE_TILE, :],
                                    preferred_element_type=F32)
            y_ref[pl.ds(start, size), :] = acc

        cascade(down_proj)


def _moe(h2, src_sorted, sb_expert, sb_rows, sb_off, sb_out, w_gate_up, b_gate_up, w_down, b_down):
    _, d = h2.shape
    n_sb = sb_expert.shape[0]
    e, f_dim = w_down.shape[0], w_down.shape[1]
    n_f = f_dim // MOE_TILE
    n_o = d // MOE_TILE

    def f_tile(s, t, nv):
        return jnp.where(nv[s] > 0, jnp.minimum(t, n_f - 1), n_f - 1)

    def o_tile(s, t, nv):
        return jnp.where(nv[s] > 0, jnp.maximum(t - n_f, 0), n_o - 1)

    grid_spec = pltpu.PrefetchScalarGridSpec(
        num_scalar_prefetch=5,
        grid=(n_sb, n_f + n_o),
        in_specs=[
            pl.BlockSpec(memory_space=pl.ANY),
            pl.BlockSpec((None, d, MOE_TILE), lambda s, t, sr, ex, nv, of, ob: (ex[s], 0, f_tile(s, t, nv))),
            pl.BlockSpec((None, d, MOE_TILE), lambda s, t, sr, ex, nv, of, ob: (ex[s], 0, n_f + f_tile(s, t, nv))),
            pl.BlockSpec((None, 1, MOE_TILE), lambda s, t, sr, ex, nv, of, ob: (ex[s], 0, f_tile(s, t, nv))),
            pl.BlockSpec((None, 1, MOE_TILE), lambda s, t, sr, ex, nv, of, ob: (ex[s], 0, n_f + f_tile(s, t, nv))),
            pl.BlockSpec((None, f_dim, MOE_TILE), lambda s, t, sr, ex, nv, of, ob: (ex[s], 0, o_tile(s, t, nv))),
            pl.BlockSpec((None, 1, MOE_TILE), lambda s, t, sr, ex, nv, of, ob: (ex[s], 0, o_tile(s, t, nv))),
        ],
        out_specs=pl.BlockSpec(
            (MOE_ROWS, MOE_TILE),
            lambda s, t, sr, ex, nv, of, ob: (ob[s], jnp.where(nv[s] > 0, jnp.maximum(t - n_f, 0), 0))),
        scratch_shapes=[pltpu.VMEM((MOE_ROWS, d), BF16),
                        pltpu.VMEM((n_f, MOE_ROWS, MOE_TILE), BF16),
                        pltpu.VMEM((2, GATHER_ROWS, d), F32),
                        pltpu.SemaphoreType.DMA((2,))],
    )
    return pl.pallas_call(
        _moe_kernel,
        grid_spec=grid_spec,
        out_shape=jax.ShapeDtypeStruct(((n_sb + 1) * MOE_ROWS, d), F32),
        compiler_params=_cparams(("arbitrary", "arbitrary"), 58),
        name="moe",
    )(src_sorted, sb_expert, sb_rows, sb_off, sb_out, h2, w_gate_up, w_gate_up,
      b_gate_up.reshape(e, 1, -1), b_gate_up.reshape(e, 1, -1), w_down, b_down.reshape(e, 1, -1))


def _combine_kernel(dest_ref, x1_ref, gt_ref, gate_ref, gfin_ref, y_hbm, o_ref, buf, sem, *, tok_offset):
    i = pl.program_id(0)
    n_tiles = pl.num_programs(0)
    tm = x1_ref.shape[0]

    def row_copy(tile, slot, r, kk):
        row = dest_ref[(tok_offset + tile * tm + r) * TOP_K + kk]
        return pltpu.make_async_copy(y_hbm.at[pl.ds(row, 1)], buf.at[slot, kk, pl.ds(r, 1)], sem.at[slot])

    def issue(tile, slot):
        def row(r, c):
            for kk in range(TOP_K):
                row_copy(tile, slot, r, kk).start()
            return c
        lax.fori_loop(0, tm, row, 0)

    @pl.when(i == 0)
    def _():
        issue(0, 0)

    @pl.when(i + 1 < n_tiles)
    def _():
        issue(i + 1, (i + 1) % 2)

    slot = i % 2

    def wait_row(r, c):
        for kk in range(TOP_K):
            row_copy(i, slot, r, kk).wait()
        return c

    lax.fori_loop(0, tm, wait_row, 0)
    gate = gate_ref[...]
    y = jnp.zeros(x1_ref.shape, F32)
    for kk in range(TOP_K):
        y = y + gate[:, kk:kk + 1] * buf[slot, kk]
    x2 = x1_ref[...] + gt_ref[...] * y
    o_ref[...] = _rms(x2, gfin_ref[...])


def _combine(x1, mod3, y_rows, dest, gates, g_final, tm, tok_offset):
    t, d = x1.shape
    g_cnt, r, _ = mod3.shape
    tiles_per_group = (t // tm) // g_cnt
    full = pl.BlockSpec((tm, d), lambda i, ds: (i, 0))
    grid_spec = pltpu.PrefetchScalarGridSpec(
        num_scalar_prefetch=1,
        grid=(t // tm,),
        in_specs=[full,
                  pl.BlockSpec((None, r, d), lambda i, ds: (i // tiles_per_group, 0, 5)),
                  pl.BlockSpec((tm, LANES), lambda i, ds: (i, 0)),
                  pl.BlockSpec((1, d), lambda i, ds: (0, 0)),
                  pl.BlockSpec(memory_space=pl.ANY)],
        out_specs=full,
        scratch_shapes=[pltpu.VMEM((2, TOP_K, tm, d), F32), pltpu.SemaphoreType.DMA((2,))],
    )
    return pl.pallas_call(
        functools.partial(_combine_kernel, tok_offset=tok_offset),
        grid_spec=grid_spec,
        out_shape=jax.ShapeDtypeStruct((t, d), F32),
        compiler_params=_cparams(("arbitrary",), 40),
        name="combine",
    )(dest, x1, mod3, gates, g_final.reshape(1, d), y_rows)


def _route(top_idx):
    n = top_idx.shape[0]
    n_assign = n * TOP_K
    n_sb = n_assign // MOE_ROWS + N_EXPERTS
    e_flat = top_idx.reshape(-1)
    onehot = (e_flat[:, None] == jnp.arange(N_EXPERTS)[None, :]).astype(I32)
    csum = jnp.cumsum(onehot, axis=0)
    counts = csum[-1]
    rank = jnp.take_along_axis(csum, e_flat[:, None], axis=1)[:, 0] - 1
    seg_start = jnp.cumsum(counts) - counts
    src_sorted = (jnp.argsort(e_flat, stable=True) // TOP_K).astype(I32)
    sb_per_e = (counts + MOE_ROWS - 1) // MOE_ROWS
    sb_end = jnp.cumsum(sb_per_e)
    sb_start = sb_end - sb_per_e
    n_used = sb_end[-1]
    dest = (sb_start[e_flat] * MOE_ROWS + rank).astype(I32)
    sb = jnp.arange(n_sb)
    sb_clamped = jnp.minimum(sb, n_used - 1)
    sb_expert = jnp.minimum(jnp.searchsorted(sb_end, sb_clamped, side='right'), N_EXPERTS - 1).astype(I32)
    local = sb_clamped - sb_start[sb_expert]
    sb_rows = jnp.where(sb < n_used, jnp.clip(counts[sb_expert] - local * MOE_ROWS, 0, MOE_ROWS), 0).astype(I32)
    sb_off = (seg_start[sb_expert] + local * MOE_ROWS).astype(I32)
    sb_out = jnp.concatenate([jnp.where(sb < n_used, sb, n_sb), (n_used - 1)[None]]).astype(I32)
    return dest, src_sorted, sb_expert, sb_rows, sb_off, sb_out


def kernel(x_prompt, x_sample, cache_k, cache_v, page_table, c_prompt, c_sample, rel_bias_table, w_ada, b_ada,
           g_mix, w_in, g_sgu, w_sp, b_sp, g_out_attn, g_out_sgu, w_out, g_ffn, w_router, b_router,
           w_gate_up, b_gate_up, w_down, b_down, g_final):
    batch, seq, d = x_prompt.shape
    dec_batch, dec_seq, _ = x_sample.shape
    depth = w_ada.shape[0]
    assert depth == 1 and d == D_MODEL and seq % MOBA_BLOCK == 0 and seq // MOBA_BLOCK <= LANES
    n_p = batch * seq
    n_s = dec_batch * dec_seq
    past_len = page_table.shape[1] * PAGE_SIZE
    assert past_len % MOBA_BLOCK == 0 and dec_seq <= SGU_CHUNK and SGU_CHUNK % dec_seq == 0
    width = N_HEADS * HEAD_DIM
    drop = lambda a: a.reshape(a.shape[1:])

    mod = _adaln(jnp.concatenate([c_prompt, c_sample], axis=0), drop(w_ada), drop(b_ada))
    mod_p = mod[:batch].reshape(batch, 1, 6 * d)
    tm_s = min(n_s, 256)
    assert n_s % tm_s == 0 and tm_s % SGU_CHUNK == 0
    mod_s = jnp.repeat(mod[batch:], dec_seq, axis=0).reshape(n_s // tm_s, tm_s, 6 * d)

    w_in_bf = drop(w_in).astype(BF16)
    w_out_bf = drop(w_out).astype(BF16)
    xp = x_prompt.reshape(n_p, d)
    xs = x_sample.reshape(n_s, d)
    g_sgu_flat = g_sgu.reshape(-1)
    g_mix1, g_ffn1, g_oa, g_os = drop(g_mix), drop(g_ffn), drop(g_out_attn), drop(g_out_sgu)
    w_sp1, b_sp1 = drop(w_sp), drop(b_sp)

    q_p, k_p, v_p, u_p, vs_p = _inproj(xp, mod_p, g_mix1, w_in_bf, g_sgu_flat, 512)
    q_s, k_s, v_s, u_s, vs_s = _inproj(xs, mod_s, g_mix1, w_in_bf, g_sgu_flat, tm_s)

    attn_p = _attn_prompt(q_p, k_p, v_p, _bias_prompt(rel_bias_table), batch, seq)
    attn_s = _attn_sample(q_s.reshape(dec_batch, dec_seq, width), k_s.reshape(dec_batch, dec_seq, width),
                          v_s.reshape(dec_batch, dec_seq, width),
                          cache_k.reshape(-1, PAGE_SIZE, width), cache_v.reshape(-1, PAGE_SIZE, width),
                          page_table, _bias_sample(rel_bias_table, past_len, dec_seq)).reshape(n_s, width)

    tri = jnp.tril(jnp.ones((SGU_CHUNK, SGU_CHUNK), F32))
    reps = SGU_CHUNK // dec_seq
    seq_id = jnp.arange(SGU_CHUNK) // dec_seq
    mask_s = tri * (seq_id[:, None] == seq_id[None, :]).astype(F32)
    wmix_s = jnp.tile(w_sp1[:, :dec_seq, :dec_seq], (1, reps, reps))
    bmix_p = jnp.repeat(b_sp1.T, SGU_WIDTH // N_GROUPS, axis=1)
    bmix_s = jnp.repeat(jnp.tile(b_sp1[:, :dec_seq], (1, reps)).T, SGU_WIDTH // N_GROUPS, axis=1)
    wr_pad = jnp.pad(drop(w_router), ((0, 0), (0, LANES - N_EXPERTS))).astype(BF16)
    br_pad = jnp.pad(drop(b_router), (0, LANES - N_EXPERTS), constant_values=NEG).reshape(1, LANES)

    x1_p, h2_p, ti_p, tg_p = _mixout(attn_p, u_p, vs_p, xp, mod_p, w_sp1, tri, bmix_p, g_oa,
                                     g_os, w_out_bf, g_ffn1, wr_pad, br_pad, 256)
    x1_s, h2_s, ti_s, tg_s = _mixout(attn_s, u_s, vs_s, xs, mod_s, wmix_s, mask_s, bmix_s, g_oa,
                                     g_os, w_out_bf, g_ffn1, wr_pad, br_pad, tm_s)

    h2 = jnp.concatenate([h2_p, h2_s], axis=0)
    top_idx = jnp.concatenate([ti_p, ti_s], axis=0)[:, :TOP_K]
    dest, src_sorted, sb_expert, sb_rows, sb_off, sb_out = _route(top_idx)
    y_rows = _moe(h2, src_sorted, sb_expert, sb_rows, sb_off, sb_out,
                  drop(w_gate_up), drop(b_gate_up), drop(w_down), drop(b_down))

    y_p = _combine(x1_p, mod_p, y_rows, dest, tg_p, g_final, 256, 0)
    y_s = _combine(x1_s, mod_s, y_rows, dest, tg_s, g_final, tm_s, n_p)

    hd = (N_HEADS, HEAD_DIM)
    gd = (N_GROUPS, SGU_WIDTH // N_GROUPS)
    last_chunk = ((seq - 1) // SGU_CHUNK) * SGU_CHUNK
    return (y_p.reshape(batch, seq, d), y_s.reshape(dec_batch, dec_seq, d),
            k_p.reshape(1, batch, seq, *hd), v_p.reshape(1, batch, seq, *hd),
            k_s.reshape(1, dec_batch, dec_seq, *hd), v_s.reshape(1, dec_batch, dec_seq, *hd),
            vs_p.reshape(batch, seq, *gd)[:, last_chunk:][None], vs_s.reshape(1, dec_batch, dec_seq, *gd))
```

```python
import functools
import math

import numpy as np
import jax
import jax.numpy as jnp
from jax import lax
from jax.experimental import pallas as pl
from jax.experimental.pallas import tpu as pltpu

F32 = jnp.float32
BF16 = jnp.bfloat16
I32 = jnp.int32

D_MODEL = 2048
ATTN_WIDTH = 1024
SGU_WIDTH = 1024
HEAD_DIM = 128
N_HEADS = 8
N_GROUPS = 8
MOBA_BLOCK = 256
MOBA_TOPK = 3
SGU_CHUNK = 128
N_BUCKETS = 32
MAX_DISTANCE = 128
N_EXPERTS = 32
TOP_K = 4
SWIGLU_ALPHA = 1.702
SWIGLU_LIMIT = 7.0
RMS_EPS = 1e-5
NEG = -1e30
PICKED = -3e38
PAGE_SIZE = 128
LANES = 128
MASKED_BUCKET = N_BUCKETS

MOE_ROWS = 3072
MOE_CHUNK = 256
MOE_TILE = 256
GATHER_ROWS = 128
DMA_UNROLL = 8
BF16_SUBLANES = 16


def _cparams(sem, vmem_mb):
    return pltpu.CompilerParams(dimension_semantics=sem, vmem_limit_bytes=vmem_mb << 20)


def _rms(x, g):
    return x * lax.rsqrt(jnp.mean(x * x, axis=-1, keepdims=True) + RMS_EPS) * g


def _gelu(x):
    return 0.5 * x * (1.0 + lax.erf(x * (2.0 ** -0.5)))


def _adaln_kernel(c_ref, w_ref, b_ref, o_ref):
    c = c_ref[...]
    a = (c * jax.nn.sigmoid(c)).astype(BF16)
    o_ref[...] = jnp.dot(a, w_ref[...].astype(BF16), preferred_element_type=F32) + b_ref[...]


def _adaln(c, w_ada, b_ada):
    r, d = c.shape
    n = w_ada.shape[1]
    tn = 1024
    return pl.pallas_call(
        _adaln_kernel,
        grid=(n // tn,),
        in_specs=[pl.BlockSpec((r, d), lambda i: (0, 0)),
                  pl.BlockSpec((d, tn), lambda i: (0, i)),
                  pl.BlockSpec((1, tn), lambda i: (0, i))],
        out_specs=pl.BlockSpec((r, tn), lambda i: (0, i)),
        out_shape=jax.ShapeDtypeStruct((r, n), F32),
        compiler_params=_cparams(("arbitrary",), 40),
        name="adaln",
    )(c, w_ada, b_ada.reshape(1, n))


def _inproj_kernel(x_ref, sh_ref, sc_ref, gmix_ref, w_ref, gsgu_ref,
                   q_ref, k_ref, v_ref, u_ref, vs_ref, h_sc):
    n = pl.program_id(1)

    @pl.when(n == 0)
    def _():
        h = _rms(x_ref[...], gmix_ref[...]) * (1.0 + sc_ref[...]) + sh_ref[...]
        h_sc[...] = h.astype(BF16)

    p = jnp.dot(h_sc[...], w_ref[...], preferred_element_type=F32)

    @pl.when(n == 0)
    def _():
        q_ref[...] = p

    @pl.when(n == 1)
    def _():
        k_ref[...] = p

    @pl.when(n == 2)
    def _():
        v_ref[...] = p

    @pl.when(n == 3)
    def _():
        u_ref[...] = _gelu(p)

    @pl.when(n == 4)
    def _():
        g = _gelu(p)
        for i in range(N_GROUPS):
            sl = slice(i * LANES, (i + 1) * LANES)
            vs_ref[:, sl] = _rms(g[:, sl], gsgu_ref[:, sl])


def _inproj(x, mod3, g_mix, w_in_bf, g_sgu, tm):
    t, d = x.shape
    g_cnt, r, _ = mod3.shape
    tiles_per_group = (t // tm) // g_cnt
    w = ATTN_WIDTH
    mod_spec = lambda chunk: pl.BlockSpec((None, r, d), lambda i, n: (i // tiles_per_group, 0, chunk))
    out_spec = pl.BlockSpec((tm, w), lambda i, n: (i, 0))
    out_sds = jax.ShapeDtypeStruct((t, w), F32)
    return pl.pallas_call(
        _inproj_kernel,
        grid=(t // tm, 5),
        in_specs=[pl.BlockSpec((tm, d), lambda i, n: (i, 0)),
                  mod_spec(0), mod_spec(1),
                  pl.BlockSpec((1, d), lambda i, n: (0, 0)),
                  pl.BlockSpec((d, w), lambda i, n: (0, n)),
                  pl.BlockSpec((1, w), lambda i, n: (0, 0))],
        out_specs=[out_spec] * 5,
        out_shape=[out_sds] * 5,
        scratch_shapes=[pltpu.VMEM((tm, d), BF16)],
        compiler_params=_cparams(("parallel", "arbitrary"), 56),
        name="inproj",
    )(x, mod3, mod3, g_mix.reshape(1, d), w_in_bf, g_sgu.reshape(1, w))


def _bucket_np(dist):
    dist = np.maximum(dist, 0)
    max_exact = N_BUCKETS // 2
    dist_f = np.maximum(dist, max_exact).astype(np.float32)
    large = max_exact + (np.log(dist_f / np.float32(max_exact)) / np.float32(math.log(MAX_DISTANCE / max_exact))
                         * np.float32(N_BUCKETS - max_exact)).astype(np.int32)
    return np.where(dist < max_exact, dist, np.minimum(large, N_BUCKETS - 1)).astype(np.int32)


def _bias_prompt_kernel(tab_ref, bucket_ref, o_ref):
    h = pl.program_id(0)
    b = bucket_ref[...]
    acc = jnp.zeros(b.shape, F32)
    for kk in range(N_BUCKETS):
        acc = jnp.where(b == kk, tab_ref[kk, h], acc)
    o_ref[0] = jnp.where(b == MASKED_BUCKET, NEG, acc)


def _bias_prompt(table):
    r = np.arange(MOBA_BLOCK)[:, None]
    c = np.arange(MOBA_BLOCK)[None, :]
    own = np.where(c <= r, _bucket_np(r - c), MASKED_BUCKET)
    prev = _bucket_np(MOBA_BLOCK + r - c)
    far = _bucket_np(2 * MOBA_BLOCK + r - c)
    assert (far == far[0, 0]).all()
    buckets = jnp.asarray(np.stack([own, prev, far]).astype(np.int32))
    return pl.pallas_call(
        _bias_prompt_kernel,
        grid=(N_HEADS,),
        in_specs=[pl.BlockSpec(memory_space=pltpu.SMEM),
                  pl.BlockSpec((3, MOBA_BLOCK, MOBA_BLOCK), lambda h: (0, 0, 0))],
        out_specs=pl.BlockSpec((1, 3, MOBA_BLOCK, MOBA_BLOCK), lambda h: (h, 0, 0, 0)),
        out_shape=jax.ShapeDtypeStruct((N_HEADS, 3, MOBA_BLOCK, MOBA_BLOCK), F32),
        compiler_params=_cparams(("arbitrary",), 16),
        name="bias_prompt",
    )(table, buckets)


def _bias_sample_kernel(tabt_ref, bucket_ref, o_ref):
    b = bucket_ref[...]
    acc = jnp.zeros(b.shape, F32)
    for kk in range(N_BUCKETS):
        acc = jnp.where(b == kk, tabt_ref[:, kk:kk + 1], acc)
    o_ref[...] = jnp.where(b == MASKED_BUCKET, NEG, acc)


def _bias_sample(table, past_len, dec_seq):
    rows = dec_seq * N_HEADS
    qi = (np.arange(rows) // N_HEADS)[:, None]
    qh = (np.arange(rows) % N_HEADS)[:, None]
    col = np.arange(past_len * N_HEADS)[None, :]
    past = np.where(col % N_HEADS == qh, _bucket_np(past_len + qi - col // N_HEADS), MASKED_BUCKET)
    col = np.arange(LANES)[None, :]
    own_ok = (col % N_HEADS == qh) & (col // N_HEADS <= qi)
    own = np.where(own_ok, _bucket_np(qi - col // N_HEADS), MASKED_BUCKET)
    buckets = jnp.asarray(np.concatenate([past, own], axis=1).astype(np.int32))
    tab_t = jnp.tile(table, (1, dec_seq)).T
    return pl.pallas_call(
        _bias_sample_kernel,
        out_shape=jax.ShapeDtypeStruct(buckets.shape, F32),
        name="bias_sample",
    )(tab_t, buckets)


def _attn_prompt_kernel(q_ref, k_ref, v_ref, bias_ref, o_ref, ka_sc, vb_sc, km_sc, s_sc):
    blk = MOBA_BLOCK
    seq = k_ref.shape[0]
    nb = seq // blk
    mrows = km_sc.shape[0]
    scale = HEAD_DIM ** -0.5
    nt = (((1,), (1,)), ((), ()))

    k = k_ref[...]
    ka_sc[:, :HEAD_DIM] = k.astype(BF16)
    row_blk = lax.broadcasted_iota(I32, (seq, LANES), 0) // blk
    ka_sc[:, HEAD_DIM:] = (lax.broadcasted_iota(I32, (seq, LANES), 1) == row_blk).astype(BF16)
    vb_sc[...] = v_ref[...].astype(BF16)
    means = [jnp.mean(k[n * blk:(n + 1) * blk, :], axis=0, keepdims=True) for n in range(nb)]
    means.append(jnp.zeros((mrows - nb, HEAD_DIM), F32))
    km_sc[...] = jnp.concatenate(means, axis=0).astype(BF16)
    sub = lax.broadcasted_iota(I32, (mrows, blk), 0).astype(F32)

    def rows(n):
        return slice(n * blk, (n + 1) * blk)

    slot = 0
    for j in range(nb):
        qb = q_ref[rows(j), :].astype(BF16)
        if j > MOBA_TOPK:
            gates = lax.dot_general(km_sc[...], qb, nt, preferred_element_type=F32)
            gates = jnp.where(sub < float(j), gates, NEG)
            allow = jnp.full((mrows, blk), NEG, F32)
            for _ in range(MOBA_TOPK):
                best = jnp.max(gates, axis=0, keepdims=True)
                first = jnp.min(jnp.where(gates == best, sub, float(mrows)), axis=0, keepdims=True)
                hit = sub == first
                allow = jnp.where(hit, 0.0, allow)
                gates = jnp.where(hit, PICKED, gates)
            allow = jnp.concatenate([allow, jnp.full((LANES - mrows, blk), NEG, F32)], axis=0).T
            q_past = jnp.concatenate([qb, allow.astype(BF16)], axis=1)
            k_cols = slice(0, HEAD_DIM + LANES)
        else:
            q_past = qb
            k_cols = slice(0, HEAD_DIM)

        s_own = lax.dot_general(qb, ka_sc[rows(j), :HEAD_DIM], nt, preferred_element_type=F32) * scale
        s_own = s_own + bias_ref[0, 0]
        run = s_own
        base = slot
        for n in range(j):
            s = lax.dot_general(q_past, ka_sc[rows(n), k_cols], nt, preferred_element_type=F32) * scale
            s = s + bias_ref[0, min(j - n, 2)]
            s_sc[base + n] = s
            run = jnp.maximum(run, s)
        slot += j
        m = jnp.max(run, axis=-1, keepdims=True)
        p = jnp.exp(s_own - m)
        psum = p
        acc = jnp.dot(p.astype(BF16), vb_sc[rows(j), :], preferred_element_type=F32)
        for n in range(j):
            p = jnp.exp(s_sc[base + n] - m)
            psum = psum + p
            acc = acc + jnp.dot(p.astype(BF16), vb_sc[rows(n), :], preferred_element_type=F32)
        o_ref[rows(j), :] = acc / jnp.sum(psum, axis=-1, keepdims=True)


def _attn_prompt(q, k, v, bias_tiles, batch, seq):
    nb = seq // MOBA_BLOCK
    spec = pl.BlockSpec((seq, HEAD_DIM), lambda b, h: (b, h))
    return pl.pallas_call(
        _attn_prompt_kernel,
        grid=(batch, N_HEADS),
        in_specs=[spec, spec, spec,
                  pl.BlockSpec((1, 3, MOBA_BLOCK, MOBA_BLOCK), lambda b, h: (h, 0, 0, 0))],
        out_specs=spec,
        out_shape=jax.ShapeDtypeStruct(q.shape, F32),
        scratch_shapes=[pltpu.VMEM((seq, HEAD_DIM + LANES), BF16), pltpu.VMEM((seq, HEAD_DIM), BF16),
                        pltpu.VMEM((pl.cdiv(nb, BF16_SUBLANES) * BF16_SUBLANES, HEAD_DIM), BF16),
                        pltpu.VMEM((max(nb * (nb - 1) // 2, 1), MOBA_BLOCK, MOBA_BLOCK), F32)],
        compiler_params=_cparams(("parallel", "parallel"), 40),
        name="attn_prompt",
    )(q, k, v, bias_tiles)


PAGES_PER_STEP = 8


def _attn_sample_kernel(pt_ref, q_ref, kn_ref, vn_ref, bias_ref, *refs, n_pages):
    del pt_ref
    pps = PAGES_PER_STEP
    k_refs = refs[:pps]
    v_refs = refs[pps:2 * pps]
    o_ref, s_sc, acc_sc, den_sc = refs[2 * pps:]
    s = pl.program_id(1)
    k_steps = n_pages // pps
    rows = q_ref.shape[0]
    cols = PAGE_SIZE * N_HEADS
    n_blocks = n_pages * PAGE_SIZE // MOBA_BLOCK
    ppb = MOBA_BLOCK // PAGE_SIZE
    scale = HEAD_DIM ** -0.5
    nt = (((1,), (1,)), ((), ()))

    @pl.when(s < k_steps)
    def _():
        qb = q_ref[...].astype(BF16)
        for pg in range(pps):
            kp = k_refs[pg][...].astype(BF16)
            s_sc[s * pps + pg] = lax.dot_general(qb, kp, nt, preferred_element_type=F32)

    @pl.when(s == k_steps - 1)
    def _():
        same_head = (lax.broadcasted_iota(I32, (rows, cols), 1) % N_HEADS
                     == lax.broadcasted_iota(I32, (rows, cols), 0) % N_HEADS)
        gates = []
        for n in range(n_blocks):
            tot = s_sc[n * ppb]
            for t in range(1, ppb):
                tot = tot + s_sc[n * ppb + t]
            gates.append(jnp.sum(jnp.where(same_head, tot, 0.0), axis=-1, keepdims=True))
        allow = []
        for n in range(n_blocks):
            ahead = jnp.zeros((rows, 1), F32)
            for mm in range(n_blocks):
                if mm == n:
                    continue
                before = (gates[mm] >= gates[n]) if mm < n else (gates[mm] > gates[n])
                ahead = ahead + before.astype(F32)
            allow.append(jnp.where(ahead < MOBA_TOPK, 0.0, NEG))
        qb = q_ref[...].astype(BF16)
        past_cols = n_pages * cols
        own = lax.dot_general(qb, kn_ref[...].astype(BF16), nt, preferred_element_type=F32) * scale
        own = own + bias_ref[:, past_cols:]
        run = jnp.full((rows, cols), NEG, F32)
        for t in range(n_pages):
            lg = s_sc[t] * scale + bias_ref[:, t * cols:(t + 1) * cols] + allow[t // ppb]
            s_sc[t] = lg
            run = jnp.maximum(run, lg)
        m = jnp.maximum(jnp.max(own, axis=-1, keepdims=True), jnp.max(run, axis=-1, keepdims=True))
        p_own = jnp.exp(own - m)
        psum = jnp.zeros((rows, cols), F32)
        for t in range(n_pages):
            p = jnp.exp(s_sc[t] - m)
            s_sc[t] = p
            psum = psum + p
        den = jnp.sum(p_own, axis=-1, keepdims=True) + jnp.sum(psum, axis=-1, keepdims=True)
        den_sc[...] = jnp.broadcast_to(den, den_sc.shape)
        acc_sc[...] = jnp.dot(p_own.astype(BF16), vn_ref[...].astype(BF16), preferred_element_type=F32)

    @pl.when(s >= k_steps)
    def _():
        acc = acc_sc[...]
        for pg in range(pps):
            p = s_sc[(s - k_steps) * pps + pg].astype(BF16)
            acc = acc + jnp.dot(p, v_refs[pg][...].astype(BF16), preferred_element_type=F32)
        acc_sc[...] = acc

    @pl.when(s == 2 * k_steps - 1)
    def _():
        o_ref[...] = acc_sc[...] / den_sc[...]


def _attn_sample(q, k_new, v_new, cache_k, cache_v, page_table, bias_s):
    b, rows, hd = q.shape
    n_pages = page_table.shape[1]
    pps = PAGES_PER_STEP
    assert n_pages % pps == 0
    k_steps = n_pages // pps
    cols = PAGE_SIZE * N_HEADS
    q_spec = pl.BlockSpec((None, rows, hd), lambda i, s, pt: (i, 0, 0))
    new_spec = pl.BlockSpec((None, LANES, hd), lambda i, s, pt: (i, 0, 0))

    def k_spec(pg):
        return pl.BlockSpec((None, cols, hd),
                            lambda i, s, pt: (pt[i * n_pages + jnp.minimum(s, k_steps - 1) * pps + pg], 0, 0))

    def v_spec(pg):
        return pl.BlockSpec((None, cols, hd),
                            lambda i, s, pt: (pt[i * n_pages + jnp.maximum(s - k_steps, 0) * pps + pg], 0, 0))

    grid_spec = pltpu.PrefetchScalarGridSpec(
        num_scalar_prefetch=1,
        grid=(b, 2 * k_steps),
        in_specs=[q_spec, new_spec, new_spec,
                  pl.BlockSpec(bias_s.shape, lambda i, s, pt: (0, 0))]
                 + [k_spec(pg) for pg in range(pps)] + [v_spec(pg) for pg in range(pps)],
        out_specs=q_spec,
        scratch_shapes=[pltpu.VMEM((n_pages, rows, cols), F32),
                        pltpu.VMEM((rows, hd), F32),
                        pltpu.VMEM((rows, hd), F32)],
    )
    return pl.pallas_call(
        functools.partial(_attn_sample_kernel, n_pages=n_pages),
        grid_spec=grid_spec,
        out_shape=jax.ShapeDtypeStruct(q.shape, F32),
        compiler_params=_cparams(("parallel", "arbitrary"), 48),
        name="attn_sample",
    )(page_table.reshape(-1), q, k_new, v_new, bias_s, *([cache_k] * pps), *([cache_v] * pps))


def _mixout_kernel(attn_ref, u_ref, vs_ref, x_ref, gt_ref, sc_ref, sh_ref, wmix_ref, mask_ref, bmix_ref,
                   ga_ref, gs_ref, wout_ref, gffn_ref, wr_ref, br_ref,
                   x1_ref, h2_ref, topi_ref, topg_ref, sgu_sc):
    tm = x_ref.shape[0]
    for g in range(N_GROUPS):
        cols = slice(g * LANES, (g + 1) * LANES)
        wg = (wmix_ref[g] * mask_ref[...]).astype(BF16)
        for c in range(tm // SGU_CHUNK):
            rws = slice(c * SGU_CHUNK, (c + 1) * SGU_CHUNK)
            mixed = jnp.dot(wg, vs_ref[rws, cols].astype(BF16), preferred_element_type=F32) + bmix_ref[:, cols]
            sgu_sc[rws, cols] = u_ref[rws, cols] * mixed
    an = _rms(attn_ref[...], ga_ref[...]).astype(BF16)
    sn = _rms(sgu_sc[...], gs_ref[...]).astype(BF16)
    out = (jnp.dot(an, wout_ref[:ATTN_WIDTH, :], preferred_element_type=F32)
           + jnp.dot(sn, wout_ref[ATTN_WIDTH:, :], preferred_element_type=F32))
    x1 = x_ref[...] + gt_ref[...] * out
    x1_ref[...] = x1
    h2 = _rms(x1, gffn_ref[...]) * (1.0 + sc_ref[...]) + sh_ref[...]
    h2_ref[...] = h2
    logits = jnp.dot(h2.astype(BF16), wr_ref[...], preferred_element_type=F32) + br_ref[...]
    lane = lax.broadcasted_iota(I32, logits.shape, 1).astype(F32)
    idx_out = jnp.zeros(logits.shape, F32)
    val_out = jnp.zeros(logits.shape, F32)
    top = None
    den = None
    for r in range(TOP_K):
        best = jnp.max(logits, axis=-1, keepdims=True)
        first = jnp.min(jnp.where(logits == best, lane, float(LANES)), axis=-1, keepdims=True)
        if r == 0:
            top = best
        e = jnp.exp(best - top)
        den = e if r == 0 else den + e
        idx_out = jnp.where(lane == float(r), first, idx_out)
        val_out = jnp.where(lane == float(r), e, val_out)
        logits = jnp.where(lane == first, PICKED, logits)
    topi_ref[...] = idx_out.astype(I32)
    topg_ref[...] = val_out / den


def _mixout(attn, u, vs, x, mod3, wmix, mixmask, bmix, g_out_attn, g_out_sgu, w_out_bf, g_ffn, wr_pad, br_pad, tm):
    t, d = x.shape
    g_cnt, r, _ = mod3.shape
    tiles_per_group = (t // tm) // g_cnt
    w = ATTN_WIDTH
    mod_spec = lambda chunk: pl.BlockSpec((None, r, d), lambda i: (i // tiles_per_group, 0, chunk))
    half = pl.BlockSpec((tm, w), lambda i: (i, 0))
    full = pl.BlockSpec((tm, d), lambda i: (i, 0))
    small = pl.BlockSpec((tm, LANES), lambda i: (i, 0))
    const = lambda shape: pl.BlockSpec(shape, lambda i: (0,) * len(shape))
    return pl.pallas_call(
        _mixout_kernel,
        grid=(t // tm,),
        in_specs=[half, half, half, full, mod_spec(2), mod_spec(4), mod_spec(3),
                  const(wmix.shape), const(mixmask.shape), const(bmix.shape),
                  const((1, w)), const((1, w)), const((d, d)), const((1, d)),
                  const((d, LANES)), const((1, LANES))],
        out_specs=[full, full, small, small],
        out_shape=[jax.ShapeDtypeStruct((t, d), F32), jax.ShapeDtypeStruct((t, d), F32),
                   jax.ShapeDtypeStruct((t, LANES), I32), jax.ShapeDtypeStruct((t, LANES), F32)],
        scratch_shapes=[pltpu.VMEM((tm, w), F32)],
        compiler_params=_cparams(("parallel",), 48),
        name="mixout",
    )(attn, u, vs, x, mod3, mod3, mod3, wmix, mixmask, bmix, g_out_attn.reshape(1, w), g_out_sgu.reshape(1, w),
      w_out_bf, g_ffn.reshape(1, d), wr_pad, br_pad)


def _moe_kernel(src_ref, exp_ref, nv_ref, off_ref, oblk_ref, h_hbm, wg_ref, wu_ref, bg_ref, bu_ref, wd_ref, bd_ref,
                y_ref, x_sc, act_sc, stage_sc, sem):
    del exp_ref, oblk_ref
    sb = pl.program_id(0)
    t = pl.program_id(1)
    n_f = act_sc.shape[0]
    nv = nv_ref[sb]
    n_chunks = (nv + MOE_CHUNK - 1) // MOE_CHUNK

    @pl.when((t == 0) & (nv > 0))
    def _():
        off = off_ref[sb]
        last = off + nv - 1
        n_batches = n_chunks * (MOE_CHUNK // GATHER_ROWS)

        def row_copy(bt, slot, r):
            tok = src_ref[jnp.minimum(off + bt * GATHER_ROWS + r, last)]
            return pltpu.make_async_copy(h_hbm.at[pl.ds(tok, 1)], stage_sc.at[slot, pl.ds(r, 1)], sem.at[slot])

        def issue(bt, slot):
            def row(r, c):
                row_copy(bt, slot, r).start()
                return c
            lax.fori_loop(0, GATHER_ROWS, row, 0, unroll=DMA_UNROLL)

        def drain(bt, slot):
            def row(r, c):
                row_copy(bt, slot, r).wait()
                return c
            lax.fori_loop(0, GATHER_ROWS, row, 0, unroll=DMA_UNROLL)
            x_sc[pl.ds(pl.multiple_of(bt * GATHER_ROWS, GATHER_ROWS), GATHER_ROWS), :] = stage_sc[slot].astype(BF16)

        issue(0, 0)

        def body(bt, c):
            slot = bt % 2

            @pl.when(bt + 1 < n_batches)
            def _():
                issue(bt + 1, 1 - slot)

            drain(bt, slot)
            return c

        lax.fori_loop(0, n_batches, body, 0)

    def cascade(fn):
        big = 4 * MOE_CHUNK
        n_big = n_chunks // 4

        def body(i, c):
            fn(pl.multiple_of(i * big, big), big)
            return c

        lax.fori_loop(0, n_big, body, 0)
        rem = n_chunks - n_big * 4
        base = n_big * big

        @pl.when(rem >= 2)
        def _():
            fn(pl.multiple_of(base, MOE_CHUNK), 2 * MOE_CHUNK)

        @pl.when(rem % 2 == 1)
        def _():
            fn(pl.multiple_of(base + (rem // 2) * (2 * MOE_CHUNK), MOE_CHUNK), MOE_CHUNK)

    @pl.when(t < n_f)
    def _():
        wg = wg_ref[...].astype(BF16)
        wu = wu_ref[...].astype(BF16)
        bg = bg_ref[...]
        bu = bu_ref[...]

        def up_proj(start, size):
            xb = x_sc[pl.ds(start, size), :]
            g = jnp.dot(xb, wg, preferred_element_type=F32) + bg
            up = jnp.dot(xb, wu, preferred_element_type=F32) + bu
            g = jnp.minimum(g, SWIGLU_LIMIT)
            up = jnp.clip(up, -SWIGLU_LIMIT, SWIGLU_LIMIT)
            act = g * jax.nn.sigmoid(SWIGLU_ALPHA * g) * (up + 1.0)
            act_sc[t, pl.ds(start, size), :] = act.astype(BF16)

        cascade(up_proj)

    @pl.when(t >= n_f)
    def _():
        wd = wd_ref[...].astype(BF16)
        bd = bd_ref[...]

        def down_proj(start, size):
            acc = jnp.zeros((size, MOE_TILE), F32) + bd
            for f in range(n_f):
                acc = acc + jnp.dot(act_sc[f, pl.ds(start, size), :], wd[f * MOE_TILE:(f + 1) * MOE_TILE, :],
                                    preferred_element_type=F32)
            y_ref[pl.ds(start, size), :] = acc

        cascade(down_proj)


def _moe(h2, src_sorted, sb_expert, sb_rows, sb_off, sb_out, w_gate_up, b_gate_up, w_down, b_down):
    _, d = h2.shape
    n_sb = sb_expert.shape[0]
    e, f_dim = w_down.shape[0], w_down.shape[1]
    n_f = f_dim // MOE_TILE
    n_o = d // MOE_TILE

    def f_tile(s, t, nv):
        return jnp.where(nv[s] > 0, jnp.minimum(t, n_f - 1), n_f - 1)

    def o_tile(s, t, nv):
        return jnp.where(nv[s] > 0, jnp.maximum(t - n_f, 0), n_o - 1)

    grid_spec = pltpu.PrefetchScalarGridSpec(
        num_scalar_prefetch=5,
        grid=(n_sb, n_f + n_o),
        in_specs=[
            pl.BlockSpec(memory_space=pl.ANY),
            pl.BlockSpec((None, d, MOE_TILE), lambda s, t, sr, ex, nv, of, ob: (ex[s], 0, f_tile(s, t, nv))),
            pl.BlockSpec((None, d, MOE_TILE), lambda s, t, sr, ex, nv, of, ob: (ex[s], 0, n_f + f_tile(s, t, nv))),
            pl.BlockSpec((None, 1, MOE_TILE), lambda s, t, sr, ex, nv, of, ob: (ex[s], 0, f_tile(s, t, nv))),
            pl.BlockSpec((None, 1, MOE_TILE), lambda s, t, sr, ex, nv, of, ob: (ex[s], 0, n_f + f_tile(s, t, nv))),
            pl.BlockSpec((None, f_dim, MOE_TILE), lambda s, t, sr, ex, nv, of, ob: (ex[s], 0, o_tile(s, t, nv))),
            pl.BlockSpec((None, 1, MOE_TILE), lambda s, t, sr, ex, nv, of, ob: (ex[s], 0, o_tile(s, t, nv))),
        ],
        out_specs=pl.BlockSpec(
            (MOE_ROWS, MOE_TILE),
            lambda s, t, sr, ex, nv, of, ob: (ob[s], jnp.where(nv[s] > 0, jnp.maximum(t - n_f, 0), 0))),
        scratch_shapes=[pltpu.VMEM((MOE_ROWS, d), BF16),
                        pltpu.VMEM((n_f, MOE_ROWS, MOE_TILE), BF16),
                        pltpu.VMEM((2, GATHER_ROWS, d), F32),
                        pltpu.SemaphoreType.DMA((2,))],
    )
    return pl.pallas_call(
        _moe_kernel,
        grid_spec=grid_spec,
        out_shape=jax.ShapeDtypeStruct(((n_sb + 1) * MOE_ROWS, d), F32),
        compiler_params=_cparams(("arbitrary", "arbitrary"), 58),
        name="moe",
    )(src_sorted, sb_expert, sb_rows, sb_off, sb_out, h2, w_gate_up, w_gate_up,
      b_gate_up.reshape(e, 1, -1), b_gate_up.reshape(e, 1, -1), w_down, b_down.reshape(e, 1, -1))


def _combine_kernel(dest_ref, x1_ref, gt_ref, gate_ref, gfin_ref, y_hbm, o_ref, buf, sem, *, tok_offset):
    i = pl.program_id(0)
    n_tiles = pl.num_programs(0)
    tm = x1_ref.shape[0]

    def row_copy(tile, slot, r, kk):
        row = dest_ref[(tok_offset + tile * tm + r) * TOP_K + kk]
        return pltpu.make_async_copy(y_hbm.at[pl.ds(row, 1)], buf.at[slot, kk, pl.ds(r, 1)], sem.at[slot])

    def issue(tile, slot):
        def row(r, c):
            for kk in range(TOP_K):
                row_copy(tile, slot, r, kk).start()
            return c
        lax.fori_loop(0, tm, row, 0, unroll=DMA_UNROLL // 2)

    @pl.when(i == 0)
    def _():
        issue(0, 0)

    @pl.when(i + 1 < n_tiles)
    def _():
        issue(i + 1, (i + 1) % 2)

    slot = i % 2

    def wait_row(r, c):
        for kk in range(TOP_K):
            row_copy(i, slot, r, kk).wait()
        return c

    lax.fori_loop(0, tm, wait_row, 0, unroll=DMA_UNROLL // 2)
    gate = gate_ref[...]
    y = jnp.zeros(x1_ref.shape, F32)
    for kk in range(TOP_K):
        y = y + gate[:, kk:kk + 1] * buf[slot, kk]
    x2 = x1_ref[...] + gt_ref[...] * y
    o_ref[...] = _rms(x2, gfin_ref[...])


def _combine(x1, mod3, y_rows, dest, gates, g_final, tm, tok_offset):
    t, d = x1.shape
    g_cnt, r, _ = mod3.shape
    tiles_per_group = (t // tm) // g_cnt
    full = pl.BlockSpec((tm, d), lambda i, ds: (i, 0))
    grid_spec = pltpu.PrefetchScalarGridSpec(
        num_scalar_prefetch=1,
        grid=(t // tm,),
        in_specs=[full,
                  pl.BlockSpec((None, r, d), lambda i, ds: (i // tiles_per_group, 0, 5)),
                  pl.BlockSpec((tm, LANES), lambda i, ds: (i, 0)),
                  pl.BlockSpec((1, d), lambda i, ds: (0, 0)),
                  pl.BlockSpec(memory_space=pl.ANY)],
        out_specs=full,
        scratch_shapes=[pltpu.VMEM((2, TOP_K, tm, d), F32), pltpu.SemaphoreType.DMA((2,))],
    )
    return pl.pallas_call(
        functools.partial(_combine_kernel, tok_offset=tok_offset),
        grid_spec=grid_spec,
        out_shape=jax.ShapeDtypeStruct((t, d), F32),
        compiler_params=_cparams(("arbitrary",), 40),
        name="combine",
    )(dest, x1, mod3, gates, g_final.reshape(1, d), y_rows)


def _route(top_idx):
    n = top_idx.shape[0]
    n_assign = n * TOP_K
    n_sb = n_assign // MOE_ROWS + N_EXPERTS
    e_flat = top_idx.reshape(-1)
    onehot = (e_flat[:, None] == jnp.arange(N_EXPERTS)[None, :]).astype(I32)
    csum = jnp.cumsum(onehot, axis=0)
    counts = csum[-1]
    rank = jnp.take_along_axis(csum, e_flat[:, None], axis=1)[:, 0] - 1
    seg_start = jnp.cumsum(counts) - counts
    src_sorted = (jnp.argsort(e_flat, stable=True) // TOP_K).astype(I32)
    sb_per_e = (counts + MOE_ROWS - 1) // MOE_ROWS
    sb_end = jnp.cumsum(sb_per_e)
    sb_start = sb_end - sb_per_e
    n_used = sb_end[-1]
    dest = (sb_start[e_flat] * MOE_ROWS + rank).astype(I32)
    sb = jnp.arange(n_sb)
    sb_clamped = jnp.minimum(sb, n_used - 1)
    sb_expert = jnp.minimum(jnp.searchsorted(sb_end, sb_clamped, side='right'), N_EXPERTS - 1).astype(I32)
    local = sb_clamped - sb_start[sb_expert]
    sb_rows = jnp.where(sb < n_used, jnp.clip(counts[sb_expert] - local * MOE_ROWS, 0, MOE_ROWS), 0).astype(I32)
    sb_off = (seg_start[sb_expert] + local * MOE_ROWS).astype(I32)
    sb_out = jnp.concatenate([jnp.where(sb < n_used, sb, n_sb), (n_used - 1)[None]]).astype(I32)
    return dest, src_sorted, sb_expert, sb_rows, sb_off, sb_out


def kernel(x_prompt, x_sample, cache_k, cache_v, page_table, c_prompt, c_sample, rel_bias_table, w_ada, b_ada,
           g_mix, w_in, g_sgu, w_sp, b_sp, g_out_attn, g_out_sgu, w_out, g_ffn, w_router, b_router,
           w_gate_up, b_gate_up, w_down, b_down, g_final):
    batch, seq, d = x_prompt.shape
    dec_batch, dec_seq, _ = x_sample.shape
    depth = w_ada.shape[0]
    assert depth == 1 and d == D_MODEL and seq % MOBA_BLOCK == 0 and seq // MOBA_BLOCK <= LANES
    n_p = batch * seq
    n_s = dec_batch * dec_seq
    past_len = page_table.shape[1] * PAGE_SIZE
    assert past_len % MOBA_BLOCK == 0 and dec_seq <= SGU_CHUNK and SGU_CHUNK % dec_seq == 0
    width = N_HEADS * HEAD_DIM
    drop = lambda a: a.reshape(a.shape[1:])

    mod = _adaln(jnp.concatenate([c_prompt, c_sample], axis=0), drop(w_ada), drop(b_ada))
    mod_p = mod[:batch].reshape(batch, 1, 6 * d)
    tm_s = min(n_s, 256)
    assert n_s % tm_s == 0 and tm_s % SGU_CHUNK == 0
    mod_s = jnp.repeat(mod[batch:], dec_seq, axis=0).reshape(n_s // tm_s, tm_s, 6 * d)

    w_in_bf = drop(w_in).astype(BF16)
    w_out_bf = drop(w_out).astype(BF16)
    xp = x_prompt.reshape(n_p, d)
    xs = x_sample.reshape(n_s, d)
    g_sgu_flat = g_sgu.reshape(-1)
    g_mix1, g_ffn1, g_oa, g_os = drop(g_mix), drop(g_ffn), drop(g_out_attn), drop(g_out_sgu)
    w_sp1, b_sp1 = drop(w_sp), drop(b_sp)

    q_p, k_p, v_p, u_p, vs_p = _inproj(xp, mod_p, g_mix1, w_in_bf, g_sgu_flat, 512)
    q_s, k_s, v_s, u_s, vs_s = _inproj(xs, mod_s, g_mix1, w_in_bf, g_sgu_flat, tm_s)

    attn_p = _attn_prompt(q_p, k_p, v_p, _bias_prompt(rel_bias_table), batch, seq)
    qh_rows = dec_seq * N_HEADS
    assert qh_rows <= LANES
    per_head = lambda a: a.reshape(dec_batch, qh_rows, HEAD_DIM)
    pad_new = lambda a: jnp.pad(per_head(a), ((0, 0), (0, LANES - qh_rows), (0, 0)))
    page_rows = PAGE_SIZE * N_HEADS
    attn_s = _attn_sample(per_head(q_s), pad_new(k_s), pad_new(v_s),
                          cache_k.reshape(-1, page_rows, HEAD_DIM), cache_v.reshape(-1, page_rows, HEAD_DIM),
                          page_table, _bias_sample(rel_bias_table, past_len, dec_seq)).reshape(n_s, width)

    tri = jnp.tril(jnp.ones((SGU_CHUNK, SGU_CHUNK), F32))
    reps = SGU_CHUNK // dec_seq
    seq_id = jnp.arange(SGU_CHUNK) // dec_seq
    mask_s = tri * (seq_id[:, None] == seq_id[None, :]).astype(F32)
    wmix_s = jnp.tile(w_sp1[:, :dec_seq, :dec_seq], (1, reps, reps))
    bmix_p = jnp.repeat(b_sp1.T, SGU_WIDTH // N_GROUPS, axis=1)
    bmix_s = jnp.repeat(jnp.tile(b_sp1[:, :dec_seq], (1, reps)).T, SGU_WIDTH // N_GROUPS, axis=1)
    wr_pad = jnp.pad(drop(w_router), ((0, 0), (0, LANES - N_EXPERTS))).astype(BF16)
    br_pad = jnp.pad(drop(b_router), (0, LANES - N_EXPERTS), constant_values=NEG).reshape(1, LANES)

    x1_p, h2_p, ti_p, tg_p = _mixout(attn_p, u_p, vs_p, xp, mod_p, w_sp1, tri, bmix_p, g_oa,
                                     g_os, w_out_bf, g_ffn1, wr_pad, br_pad, 256)
    x1_s, h2_s, ti_s, tg_s = _mixout(attn_s, u_s, vs_s, xs, mod_s, wmix_s, mask_s, bmix_s, g_oa,
                                     g_os, w_out_bf, g_ffn1, wr_pad, br_pad, tm_s)

    h2 = jnp.concatenate([h2_p, h2_s], axis=0)
    top_idx = jnp.concatenate([ti_p, ti_s], axis=0)[:, :TOP_K]
    dest, src_sorted, sb_expert, sb_rows, sb_off, sb_out = _route(top_idx)
    y_rows = _moe(h2, src_sorted, sb_expert, sb_rows, sb_off, sb_out,
                  drop(w_gate_up), drop(b_gate_up), drop(w_down), drop(b_down))

    y_p = _combine(x1_p, mod_p, y_rows, dest, tg_p, g_final, 256, 0)
    y_s = _combine(x1_s, mod_s, y_rows, dest, tg_s, g_final, tm_s, n_p)

    hd = (N_HEADS, HEAD_DIM)
    gd = (N_GROUPS, SGU_WIDTH // N_GROUPS)
    last_chunk = ((seq - 1) // SGU_CHUNK) * SGU_CHUNK
    return (y_p.reshape(batch, seq, d), y_s.reshape(dec_batch, dec_seq, d),
            k_p.reshape(1, batch, seq, *hd), v_p.reshape(1, batch, seq, *hd),
            k_s.reshape(1, dec_batch, dec_seq, *hd), v_s.reshape(1, dec_batch, dec_seq, *hd),
            vs_p.reshape(batch, seq, *gd)[:, last_chunk:][None], vs_s.reshape(1, dec_batch, dec_seq, *gd))
```

```python
import functools
import math

import numpy as np
import jax
import jax.numpy as jnp
from jax import lax
from jax.experimental import pallas as pl
from jax.experimental.pallas import tpu as pltpu

F32 = jnp.float32
BF16 = jnp.bfloat16
I32 = jnp.int32

D_MODEL = 2048
ATTN_WIDTH = 1024
SGU_WIDTH = 1024
HEAD_DIM = 128
N_HEADS = 8
N_GROUPS = 8
MOBA_BLOCK = 256
MOBA_TOPK = 3
SGU_CHUNK = 128
N_BUCKETS = 32
MAX_DISTANCE = 128
N_EXPERTS = 32
TOP_K = 4
SWIGLU_ALPHA = 1.702
SWIGLU_LIMIT = 7.0
RMS_EPS = 1e-5
NEG = -1e30
PICKED = -3e38
PAGE_SIZE = 128
LANES = 128
MASKED_BUCKET = N_BUCKETS

MOE_ROWS = 3072
MOE_CHUNK = 256
MOE_TILE = 256
GATHER_ROWS = 128
GATHER_SLOTS = 6
F32_SUBLANES = 8
BF16_SUBLANES = 16


def _cparams(sem, vmem_mb):
    return pltpu.CompilerParams(dimension_semantics=sem, vmem_limit_bytes=vmem_mb << 20)


def _rms(x, g):
    return x * lax.rsqrt(jnp.mean(x * x, axis=-1, keepdims=True) + RMS_EPS) * g


def _gelu(x):
    return 0.5 * x * (1.0 + lax.erf(x * (2.0 ** -0.5)))


def _adaln_kernel(c_ref, w_ref, b_ref, o_ref):
    c = c_ref[...]
    a = (c * jax.nn.sigmoid(c)).astype(BF16)
    o_ref[...] = jnp.dot(a, w_ref[...].astype(BF16), preferred_element_type=F32) + b_ref[...]


def _adaln(c, w_ada, b_ada):
    r, d = c.shape
    n = w_ada.shape[1]
    tn = 1024
    return pl.pallas_call(
        _adaln_kernel,
        grid=(n // tn,),
        in_specs=[pl.BlockSpec((r, d), lambda i: (0, 0)),
                  pl.BlockSpec((d, tn), lambda i: (0, i)),
                  pl.BlockSpec((1, tn), lambda i: (0, i))],
        out_specs=pl.BlockSpec((r, tn), lambda i: (0, i)),
        out_shape=jax.ShapeDtypeStruct((r, n), F32),
        compiler_params=_cparams(("arbitrary",), 40),
        name="adaln",
    )(c, w_ada, b_ada.reshape(1, n))


def _inproj_kernel(x_ref, sh_ref, sc_ref, gmix_ref, w_ref, gsgu_ref,
                   q_ref, k_ref, v_ref, u_ref, vs_ref, h_sc):
    n = pl.program_id(1)

    @pl.when(n == 0)
    def _():
        h = _rms(x_ref[...], gmix_ref[...]) * (1.0 + sc_ref[...]) + sh_ref[...]
        h_sc[...] = h.astype(BF16)

    p = jnp.dot(h_sc[...], w_ref[...], preferred_element_type=F32)

    @pl.when(n == 0)
    def _():
        q_ref[...] = p

    @pl.when(n == 1)
    def _():
        k_ref[...] = p

    @pl.when(n == 2)
    def _():
        v_ref[...] = p

    @pl.when(n == 3)
    def _():
        u_ref[...] = _gelu(p)

    @pl.when(n == 4)
    def _():
        g = _gelu(p)
        for i in range(N_GROUPS):
            sl = slice(i * LANES, (i + 1) * LANES)
            vs_ref[:, sl] = _rms(g[:, sl], gsgu_ref[:, sl])


def _inproj(x, mod3, g_mix, w_in_bf, g_sgu, tm):
    t, d = x.shape
    g_cnt, r, _ = mod3.shape
    tiles_per_group = (t // tm) // g_cnt
    w = ATTN_WIDTH
    mod_spec = lambda chunk: pl.BlockSpec((None, r, d), lambda i, n: (i // tiles_per_group, 0, chunk))
    out_spec = pl.BlockSpec((tm, w), lambda i, n: (i, 0))
    out_sds = jax.ShapeDtypeStruct((t, w), F32)
    return pl.pallas_call(
        _inproj_kernel,
        grid=(t // tm, 5),
        in_specs=[pl.BlockSpec((tm, d), lambda i, n: (i, 0)),
                  mod_spec(0), mod_spec(1),
                  pl.BlockSpec((1, d), lambda i, n: (0, 0)),
                  pl.BlockSpec((d, w), lambda i, n: (0, n)),
                  pl.BlockSpec((1, w), lambda i, n: (0, 0))],
        out_specs=[out_spec] * 5,
        out_shape=[out_sds] * 5,
        scratch_shapes=[pltpu.VMEM((tm, d), BF16)],
        compiler_params=_cparams(("parallel", "arbitrary"), 56),
        name="inproj",
    )(x, mod3, mod3, g_mix.reshape(1, d), w_in_bf, g_sgu.reshape(1, w))


def _bucket_np(dist):
    dist = np.maximum(dist, 0)
    max_exact = N_BUCKETS // 2
    dist_f = np.maximum(dist, max_exact).astype(np.float32)
    large = max_exact + (np.log(dist_f / np.float32(max_exact)) / np.float32(math.log(MAX_DISTANCE / max_exact))
                         * np.float32(N_BUCKETS - max_exact)).astype(np.int32)
    return np.where(dist < max_exact, dist, np.minimum(large, N_BUCKETS - 1)).astype(np.int32)


def _bias_prompt_kernel(tab_ref, bucket_ref, o_ref):
    h = pl.program_id(0)
    b = bucket_ref[...]
    acc = jnp.zeros(b.shape, F32)
    for kk in range(N_BUCKETS):
        acc = jnp.where(b == kk, tab_ref[kk, h], acc)
    o_ref[0] = jnp.where(b == MASKED_BUCKET, NEG, acc)


def _bias_prompt(table):
    r = np.arange(MOBA_BLOCK)[:, None]
    c = np.arange(MOBA_BLOCK)[None, :]
    own = np.where(c <= r, _bucket_np(r - c), MASKED_BUCKET)
    prev = _bucket_np(MOBA_BLOCK + r - c)
    far = _bucket_np(2 * MOBA_BLOCK + r - c)
    assert (far == far[0, 0]).all()
    buckets = jnp.asarray(np.stack([own, prev, far]).astype(np.int32))
    return pl.pallas_call(
        _bias_prompt_kernel,
        grid=(N_HEADS,),
        in_specs=[pl.BlockSpec(memory_space=pltpu.SMEM),
                  pl.BlockSpec((3, MOBA_BLOCK, MOBA_BLOCK), lambda h: (0, 0, 0))],
        out_specs=pl.BlockSpec((1, 3, MOBA_BLOCK, MOBA_BLOCK), lambda h: (h, 0, 0, 0)),
        out_shape=jax.ShapeDtypeStruct((N_HEADS, 3, MOBA_BLOCK, MOBA_BLOCK), F32),
        compiler_params=_cparams(("arbitrary",), 16),
        name="bias_prompt",
    )(table, buckets)


def _bias_sample_kernel(tabt_ref, bucket_ref, o_ref):
    b = bucket_ref[...]
    acc = jnp.zeros(b.shape, F32)
    for kk in range(N_BUCKETS):
        acc = jnp.where(b == kk, tabt_ref[:, kk:kk + 1], acc)
    o_ref[...] = jnp.where(b == MASKED_BUCKET, NEG, acc)


def _bias_sample(table, past_len, dec_seq):
    rows = dec_seq * N_HEADS
    qi = (np.arange(rows) // N_HEADS)[:, None]
    qh = (np.arange(rows) % N_HEADS)[:, None]
    col = np.arange(past_len * N_HEADS)[None, :]
    past = np.where(col % N_HEADS == qh, _bucket_np(past_len + qi - col // N_HEADS), MASKED_BUCKET)
    col = np.arange(LANES)[None, :]
    own_ok = (col % N_HEADS == qh) & (col // N_HEADS <= qi)
    own = np.where(own_ok, _bucket_np(qi - col // N_HEADS), MASKED_BUCKET)
    buckets = jnp.asarray(np.concatenate([past, own], axis=1).astype(np.int32))
    tab_t = jnp.tile(table, (1, dec_seq)).T
    return pl.pallas_call(
        _bias_sample_kernel,
        out_shape=jax.ShapeDtypeStruct(buckets.shape, F32),
        name="bias_sample",
    )(tab_t, buckets)


def _attn_prompt_kernel(q_ref, k_ref, v_ref, bias_ref, o_ref, ka_sc, vb_sc, km_sc, s_sc):
    blk = MOBA_BLOCK
    seq = k_ref.shape[0]
    nb = seq // blk
    mrows = km_sc.shape[0]
    scale = HEAD_DIM ** -0.5
    nt = (((1,), (1,)), ((), ()))

    k = k_ref[...]
    ka_sc[:, :HEAD_DIM] = k.astype(BF16)
    row_blk = lax.broadcasted_iota(I32, (seq, LANES), 0) // blk
    ka_sc[:, HEAD_DIM:] = (lax.broadcasted_iota(I32, (seq, LANES), 1) == row_blk).astype(BF16)
    vb_sc[...] = v_ref[...].astype(BF16)
    means = [jnp.mean(k[n * blk:(n + 1) * blk, :], axis=0, keepdims=True) for n in range(nb)]
    means.append(jnp.zeros((mrows - nb, HEAD_DIM), F32))
    km_sc[...] = jnp.concatenate(means, axis=0).astype(BF16)
    sub = lax.broadcasted_iota(I32, (mrows, blk), 0).astype(F32)

    def rows(n):
        return slice(n * blk, (n + 1) * blk)

    slot = 0
    for j in range(nb):
        qb = q_ref[rows(j), :].astype(BF16)
        if j > MOBA_TOPK:
            gates = lax.dot_general(km_sc[...], qb, nt, preferred_element_type=F32)
            gates = jnp.where(sub < float(j), gates, NEG)
            allow = jnp.full((mrows, blk), NEG, F32)
            for _ in range(MOBA_TOPK):
                best = jnp.max(gates, axis=0, keepdims=True)
                first = jnp.min(jnp.where(gates == best, sub, float(mrows)), axis=0, keepdims=True)
                hit = sub == first
                allow = jnp.where(hit, 0.0, allow)
                gates = jnp.where(hit, PICKED, gates)
            allow = jnp.concatenate([allow, jnp.full((LANES - mrows, blk), NEG, F32)], axis=0).T
            q_past = jnp.concatenate([qb, allow.astype(BF16)], axis=1)
            k_cols = slice(0, HEAD_DIM + LANES)
        else:
            q_past = qb
            k_cols = slice(0, HEAD_DIM)

        s_own = lax.dot_general(qb, ka_sc[rows(j), :HEAD_DIM], nt, preferred_element_type=F32) * scale
        s_own = s_own + bias_ref[0, 0]
        run = s_own
        base = slot
        for n in range(j):
            s = lax.dot_general(q_past, ka_sc[rows(n), k_cols], nt, preferred_element_type=F32) * scale
            s = s + bias_ref[0, min(j - n, 2)]
            s_sc[base + n] = s
            run = jnp.maximum(run, s)
        slot += j
        m = jnp.max(run, axis=-1, keepdims=True)
        p = jnp.exp(s_own - m)
        psum = p
        acc = jnp.dot(p.astype(BF16), vb_sc[rows(j), :], preferred_element_type=F32)
        for n in range(j):
            p = jnp.exp(s_sc[base + n] - m)
            psum = psum + p
            acc = acc + jnp.dot(p.astype(BF16), vb_sc[rows(n), :], preferred_element_type=F32)
        o_ref[rows(j), :] = acc / jnp.sum(psum, axis=-1, keepdims=True)


def _attn_prompt(q, k, v, bias_tiles, batch, seq):
    nb = seq // MOBA_BLOCK
    spec = pl.BlockSpec((seq, HEAD_DIM), lambda b, h: (b, h))
    return pl.pallas_call(
        _attn_prompt_kernel,
        grid=(batch, N_HEADS),
        in_specs=[spec, spec, spec,
                  pl.BlockSpec((1, 3, MOBA_BLOCK, MOBA_BLOCK), lambda b, h: (h, 0, 0, 0))],
        out_specs=spec,
        out_shape=jax.ShapeDtypeStruct(q.shape, F32),
        scratch_shapes=[pltpu.VMEM((seq, HEAD_DIM + LANES), BF16), pltpu.VMEM((seq, HEAD_DIM), BF16),
                        pltpu.VMEM((pl.cdiv(nb, BF16_SUBLANES) * BF16_SUBLANES, HEAD_DIM), BF16),
                        pltpu.VMEM((max(nb * (nb - 1) // 2, 1), MOBA_BLOCK, MOBA_BLOCK), F32)],
        compiler_params=_cparams(("parallel", "parallel"), 40),
        name="attn_prompt",
    )(q, k, v, bias_tiles)


PAGES_PER_STEP = 16


def _attn_sample_kernel(pt_ref, q_ref, kn_ref, vn_ref, bias_ref, *refs, n_pages):
    del pt_ref
    pps = PAGES_PER_STEP
    k_refs = refs[:pps]
    v_refs = refs[pps:2 * pps]
    o_ref, s_sc, acc_sc, den_sc = refs[2 * pps:]
    s = pl.program_id(1)
    k_steps = n_pages // pps
    rows = q_ref.shape[0]
    cols = PAGE_SIZE * N_HEADS
    n_blocks = n_pages * PAGE_SIZE // MOBA_BLOCK
    ppb = MOBA_BLOCK // PAGE_SIZE
    scale = HEAD_DIM ** -0.5
    nt = (((1,), (1,)), ((), ()))

    @pl.when(s < k_steps)
    def _():
        qb = q_ref[...].astype(BF16)
        for pg in range(pps):
            kp = k_refs[pg][...].astype(BF16)
            s_sc[s * pps + pg] = lax.dot_general(qb, kp, nt, preferred_element_type=F32)

    @pl.when(s == k_steps - 1)
    def _():
        same_head = (lax.broadcasted_iota(I32, (rows, cols), 1) % N_HEADS
                     == lax.broadcasted_iota(I32, (rows, cols), 0) % N_HEADS)
        gates = []
        for n in range(n_blocks):
            tot = s_sc[n * ppb]
            for t in range(1, ppb):
                tot = tot + s_sc[n * ppb + t]
            gates.append(jnp.sum(jnp.where(same_head, tot, 0.0), axis=-1, keepdims=True))
        allow = []
        for n in range(n_blocks):
            ahead = jnp.zeros((rows, 1), F32)
            for mm in range(n_blocks):
                if mm == n:
                    continue
                before = (gates[mm] >= gates[n]) if mm < n else (gates[mm] > gates[n])
                ahead = ahead + before.astype(F32)
            allow.append(jnp.where(ahead < MOBA_TOPK, 0.0, NEG))
        qb = q_ref[...].astype(BF16)
        past_cols = n_pages * cols
        own = lax.dot_general(qb, kn_ref[...].astype(BF16), nt, preferred_element_type=F32) * scale
        own = own + bias_ref[:, past_cols:]
        run = jnp.full((rows, cols), NEG, F32)
        for t in range(n_pages):
            lg = s_sc[t] * scale + bias_ref[:, t * cols:(t + 1) * cols] + allow[t // ppb]
            s_sc[t] = lg
            run = jnp.maximum(run, lg)
        m = jnp.maximum(jnp.max(own, axis=-1, keepdims=True), jnp.max(run, axis=-1, keepdims=True))
        p_own = jnp.exp(own - m)
        psum = jnp.zeros((rows, cols), F32)
        for t in range(n_pages):
            p = jnp.exp(s_sc[t] - m)
            s_sc[t] = p
            psum = psum + p
        den = jnp.sum(p_own, axis=-1, keepdims=True) + jnp.sum(psum, axis=-1, keepdims=True)
        den_sc[...] = jnp.broadcast_to(den, den_sc.shape)
        acc_sc[...] = jnp.dot(p_own.astype(BF16), vn_ref[...].astype(BF16), preferred_element_type=F32)

    @pl.when(s >= k_steps)
    def _():
        acc = acc_sc[...]
        for pg in range(pps):
            p = s_sc[(s - k_steps) * pps + pg].astype(BF16)
            acc = acc + jnp.dot(p, v_refs[pg][...].astype(BF16), preferred_element_type=F32)
        acc_sc[...] = acc

    @pl.when(s == 2 * k_steps - 1)
    def _():
        o_ref[...] = acc_sc[...] / den_sc[...]


def _attn_sample(q, k_new, v_new, cache_k, cache_v, page_table, bias_s):
    b, rows, hd = q.shape
    n_pages = page_table.shape[1]
    pps = PAGES_PER_STEP
    assert n_pages % pps == 0
    k_steps = n_pages // pps
    cols = PAGE_SIZE * N_HEADS
    q_spec = pl.BlockSpec((None, rows, hd), lambda i, s, pt: (i, 0, 0))
    new_spec = pl.BlockSpec((None, LANES, hd), lambda i, s, pt: (i, 0, 0))

    def k_spec(pg):
        return pl.BlockSpec((None, cols, hd),
                            lambda i, s, pt: (pt[i * n_pages + jnp.minimum(s, k_steps - 1) * pps + pg], 0, 0))

    def v_spec(pg):
        return pl.BlockSpec((None, cols, hd),
                            lambda i, s, pt: (pt[i * n_pages + jnp.maximum(s - k_steps, 0) * pps + pg], 0, 0))

    grid_spec = pltpu.PrefetchScalarGridSpec(
        num_scalar_prefetch=1,
        grid=(b, 2 * k_steps),
        in_specs=[q_spec, new_spec, new_spec,
                  pl.BlockSpec(bias_s.shape, lambda i, s, pt: (0, 0))]
                 + [k_spec(pg) for pg in range(pps)] + [v_spec(pg) for pg in range(pps)],
        out_specs=q_spec,
        scratch_shapes=[pltpu.VMEM((n_pages, rows, cols), F32),
                        pltpu.VMEM((rows, hd), F32),
                        pltpu.VMEM((rows, hd), F32)],
    )
    return pl.pallas_call(
        functools.partial(_attn_sample_kernel, n_pages=n_pages),
        grid_spec=grid_spec,
        out_shape=jax.ShapeDtypeStruct(q.shape, F32),
        compiler_params=_cparams(("parallel", "arbitrary"), 48),
        name="attn_sample",
    )(page_table.reshape(-1), q, k_new, v_new, bias_s, *([cache_k] * pps), *([cache_v] * pps))


def _mixout_kernel(attn_ref, u_ref, vs_ref, x_ref, gt_ref, sc_ref, sh_ref, wmix_ref, mask_ref, bmix_ref,
                   ga_ref, gs_ref, wout_ref, gffn_ref, wr_ref, br_ref,
                   x1_ref, h2_ref, topi_ref, topg_ref, sgu_sc):
    tm = x_ref.shape[0]
    for g in range(N_GROUPS):
        cols = slice(g * LANES, (g + 1) * LANES)
        wg = (wmix_ref[g] * mask_ref[...]).astype(BF16)
        for c in range(tm // SGU_CHUNK):
            rws = slice(c * SGU_CHUNK, (c + 1) * SGU_CHUNK)
            mixed = jnp.dot(wg, vs_ref[rws, cols].astype(BF16), preferred_element_type=F32) + bmix_ref[:, cols]
            sgu_sc[rws, cols] = u_ref[rws, cols] * mixed
    an = _rms(attn_ref[...], ga_ref[...]).astype(BF16)
    sn = _rms(sgu_sc[...], gs_ref[...]).astype(BF16)
    out = (jnp.dot(an, wout_ref[:ATTN_WIDTH, :], preferred_element_type=F32)
           + jnp.dot(sn, wout_ref[ATTN_WIDTH:, :], preferred_element_type=F32))
    x1 = x_ref[...] + gt_ref[...] * out
    x1_ref[...] = x1
    h2 = _rms(x1, gffn_ref[...]) * (1.0 + sc_ref[...]) + sh_ref[...]
    h2_ref[...] = h2
    logits = jnp.dot(h2.astype(BF16), wr_ref[...], preferred_element_type=F32) + br_ref[...]
    lane = lax.broadcasted_iota(I32, logits.shape, 1).astype(F32)
    idx_out = jnp.zeros(logits.shape, F32)
    val_out = jnp.zeros(logits.shape, F32)
    top = None
    den = None
    for r in range(TOP_K):
        best = jnp.max(logits, axis=-1, keepdims=True)
        first = jnp.min(jnp.where(logits == best, lane, float(LANES)), axis=-1, keepdims=True)
        if r == 0:
            top = best
        e = jnp.exp(best - top)
        den = e if r == 0 else den + e
        idx_out = jnp.where(lane == float(r), first, idx_out)
        val_out = jnp.where(lane == float(r), e, val_out)
        logits = jnp.where(lane == first, PICKED, logits)
    topi_ref[...] = idx_out.astype(I32)
    topg_ref[...] = val_out / den


def _mixout(attn, u, vs, x, mod3, wmix, mixmask, bmix, g_out_attn, g_out_sgu, w_out_bf, g_ffn, wr_pad, br_pad, tm):
    t, d = x.shape
    g_cnt, r, _ = mod3.shape
    tiles_per_group = (t // tm) // g_cnt
    w = ATTN_WIDTH
    mod_spec = lambda chunk: pl.BlockSpec((None, r, d), lambda i: (i // tiles_per_group, 0, chunk))
    half = pl.BlockSpec((tm, w), lambda i: (i, 0))
    full = pl.BlockSpec((tm, d), lambda i: (i, 0))
    small = pl.BlockSpec((tm, LANES), lambda i: (i, 0))
    const = lambda shape: pl.BlockSpec(shape, lambda i: (0,) * len(shape))
    return pl.pallas_call(
        _mixout_kernel,
        grid=(t // tm,),
        in_specs=[half, half, half, full, mod_spec(2), mod_spec(4), mod_spec(3),
                  const(wmix.shape), const(mixmask.shape), const(bmix.shape),
                  const((1, w)), const((1, w)), const((d, d)), const((1, d)),
                  const((d, LANES)), const((1, LANES))],
        out_specs=[full, full, small, small],
        out_shape=[jax.ShapeDtypeStruct((t, d), F32), jax.ShapeDtypeStruct((t, d), F32),
                   jax.ShapeDtypeStruct((t, LANES), I32), jax.ShapeDtypeStruct((t, LANES), F32)],
        scratch_shapes=[pltpu.VMEM((tm, w), F32)],
        compiler_params=_cparams(("parallel",), 48),
        name="mixout",
    )(attn, u, vs, x, mod3, mod3, mod3, wmix, mixmask, bmix, g_out_attn.reshape(1, w), g_out_sgu.reshape(1, w),
      w_out_bf, g_ffn.reshape(1, d), wr_pad, br_pad)


def _moe_kernel(src_ref, exp_ref, nv_ref, off_ref, oblk_ref, h_hbm, wg_ref, wu_ref, bg_ref, bu_ref, wd_ref, bd_ref,
                y_ref, x_sc, act_sc, stage_sc, sem):
    del exp_ref, oblk_ref
    sb = pl.program_id(0)
    t = pl.program_id(1)
    n_f = act_sc.shape[0]
    nv = nv_ref[sb]
    n_chunks = (nv + MOE_CHUNK - 1) // MOE_CHUNK

    @pl.when((t == 0) & (nv > 0))
    def _():
        off = off_ref[sb]
        last = off + nv - 1
        n_batches = n_chunks * (MOE_CHUNK // GATHER_ROWS)

        def row_copy(bt, slot, i, u):
            tok = src_ref[jnp.minimum(off + bt * GATHER_ROWS + i * F32_SUBLANES + u, last)]
            return pltpu.make_async_copy(h_hbm.at[pl.ds(tok, 1)], stage_sc.at[slot, i, pl.ds(u, 1)], sem.at[slot])

        def issue(bt, slot):
            def group(i, c):
                for u in range(F32_SUBLANES):
                    row_copy(bt, slot, i, u).start()
                return c
            lax.fori_loop(0, GATHER_ROWS // F32_SUBLANES, group, 0)

        def drain(bt, slot):
            def group(i, c):
                for u in range(F32_SUBLANES):
                    row_copy(bt, slot, i, u).wait()
                return c
            lax.fori_loop(0, GATHER_ROWS // F32_SUBLANES, group, 0)
            rows = stage_sc[slot].reshape(GATHER_ROWS, x_sc.shape[1])
            x_sc[pl.ds(pl.multiple_of(bt * GATHER_ROWS, GATHER_ROWS), GATHER_ROWS), :] = rows.astype(BF16)

        n_slots = stage_sc.shape[0]
        ahead = n_slots - 1
        for bt in range(ahead):
            @pl.when(bt < n_batches)
            def _():
                issue(bt, bt)

        def body(bt, c):
            @pl.when(bt + ahead < n_batches)
            def _():
                issue(bt + ahead, (bt + ahead) % n_slots)

            drain(bt, bt % n_slots)
            return c

        lax.fori_loop(0, n_batches, body, 0)

    def cascade(fn):
        big = 4 * MOE_CHUNK
        n_big = n_chunks // 4

        def body(i, c):
            fn(pl.multiple_of(i * big, big), big)
            return c

        lax.fori_loop(0, n_big, body, 0)
        rem = n_chunks - n_big * 4
        base = n_big * big

        @pl.when(rem >= 2)
        def _():
            fn(pl.multiple_of(base, MOE_CHUNK), 2 * MOE_CHUNK)

        @pl.when(rem % 2 == 1)
        def _():
            fn(pl.multiple_of(base + (rem // 2) * (2 * MOE_CHUNK), MOE_CHUNK), MOE_CHUNK)

    @pl.when(t < n_f)
    def _():
        bg = bg_ref[...]
        bu = bu_ref[...]

        def up_proj(start, size):
            xb = x_sc[pl.ds(start, size), :]
            g = jnp.dot(xb, wg_ref[...].astype(BF16), preferred_element_type=F32) + bg
            up = jnp.dot(xb, wu_ref[...].astype(BF16), preferred_element_type=F32) + bu
            g = jnp.minimum(g, SWIGLU_LIMIT)
            up = jnp.clip(up, -SWIGLU_LIMIT, SWIGLU_LIMIT)
            act = g * jax.nn.sigmoid(SWIGLU_ALPHA * g) * (up + 1.0)
            act_sc[t, pl.ds(start, size), :] = act.astype(BF16)

        cascade(up_proj)

    @pl.when(t >= n_f)
    def _():
        bd = bd_ref[...]

        def down_proj(start, size):
            acc = jnp.zeros((size, MOE_TILE), F32) + bd
            for f in range(n_f):
                wd = wd_ref[f * MOE_TILE:(f + 1) * MOE_TILE, :].astype(BF16)
                acc = acc + jnp.dot(act_sc[f, pl.ds(start, size), :], wd, preferred_element_type=F32)
            y_ref[pl.ds(start, size), :] = acc

        cascade(down_proj)


def _moe(h2, src_sorted, sb_expert, sb_rows, sb_off, sb_out, w_gate_up, b_gate_up, w_down, b_down):
    _, d = h2.shape
    n_sb = sb_expert.shape[0]
    e, f_dim = w_down.shape[0], w_down.shape[1]
    n_f = f_dim // MOE_TILE
    n_o = d // MOE_TILE

    def f_tile(s, t, nv):
        return jnp.where(nv[s] > 0, jnp.minimum(t, n_f - 1), n_f - 1)

    def o_tile(s, t, nv):
        return jnp.where(nv[s] > 0, jnp.maximum(t - n_f, 0), n_o - 1)

    grid_spec = pltpu.PrefetchScalarGridSpec(
        num_scalar_prefetch=5,
        grid=(n_sb, n_f + n_o),
        in_specs=[
            pl.BlockSpec(memory_space=pl.ANY),
            pl.BlockSpec((None, d, MOE_TILE), lambda s, t, sr, ex, nv, of, ob: (ex[s], 0, f_tile(s, t, nv))),
            pl.BlockSpec((None, d, MOE_TILE), lambda s, t, sr, ex, nv, of, ob: (ex[s], 0, n_f + f_tile(s, t, nv))),
            pl.BlockSpec((None, 1, MOE_TILE), lambda s, t, sr, ex, nv, of, ob: (ex[s], 0, f_tile(s, t, nv))),
            pl.BlockSpec((None, 1, MOE_TILE), lambda s, t, sr, ex, nv, of, ob: (ex[s], 0, n_f + f_tile(s, t, nv))),
            pl.BlockSpec((None, f_dim, MOE_TILE), lambda s, t, sr, ex, nv, of, ob: (ex[s], 0, o_tile(s, t, nv))),
            pl.BlockSpec((None, 1, MOE_TILE), lambda s, t, sr, ex, nv, of, ob: (ex[s], 0, o_tile(s, t, nv))),
        ],
        out_specs=pl.BlockSpec(
            (MOE_ROWS, MOE_TILE),
            lambda s, t, sr, ex, nv, of, ob: (ob[s], jnp.where(nv[s] > 0, jnp.maximum(t - n_f, 0), 0))),
        scratch_shapes=[pltpu.VMEM((MOE_ROWS, d), BF16),
                        pltpu.VMEM((n_f, MOE_ROWS, MOE_TILE), BF16),
                        pltpu.VMEM((GATHER_SLOTS, GATHER_ROWS // F32_SUBLANES, F32_SUBLANES, d), F32),
                        pltpu.SemaphoreType.DMA((GATHER_SLOTS,))],
    )
    return pl.pallas_call(
        _moe_kernel,
        grid_spec=grid_spec,
        out_shape=jax.ShapeDtypeStruct(((n_sb + 1) * MOE_ROWS, d), F32),
        compiler_params=_cparams(("arbitrary", "arbitrary"), 58),
        name="moe",
    )(src_sorted, sb_expert, sb_rows, sb_off, sb_out, h2, w_gate_up, w_gate_up,
      b_gate_up.reshape(e, 1, -1), b_gate_up.reshape(e, 1, -1), w_down, b_down.reshape(e, 1, -1))


def _combine_kernel(dest_ref, x1_ref, gt_ref, gate_ref, gfin_ref, y_hbm, o_ref, buf, sem, *, tok_offset):
    i = pl.program_id(0)
    n_tiles = pl.num_programs(0)
    tm = x1_ref.shape[0]

    groups = tm // F32_SUBLANES

    def row_copy(tile, slot, g, u, kk):
        row = dest_ref[(tok_offset + tile * tm + g * F32_SUBLANES + u) * TOP_K + kk]
        return pltpu.make_async_copy(y_hbm.at[pl.ds(row, 1)], buf.at[slot, kk, g, pl.ds(u, 1)], sem.at[slot])

    def issue(tile, slot):
        def group(g, c):
            for u in range(F32_SUBLANES):
                for kk in range(TOP_K):
                    row_copy(tile, slot, g, u, kk).start()
            return c
        lax.fori_loop(0, groups, group, 0)

    @pl.when(i == 0)
    def _():
        issue(0, 0)

    @pl.when(i + 1 < n_tiles)
    def _():
        issue(i + 1, (i + 1) % 2)

    slot = i % 2

    def wait_group(g, c):
        for u in range(F32_SUBLANES):
            for kk in range(TOP_K):
                row_copy(i, slot, g, u, kk).wait()
        return c

    lax.fori_loop(0, groups, wait_group, 0)
    gate = gate_ref[...]
    y = jnp.zeros(x1_ref.shape, F32)
    for kk in range(TOP_K):
        y = y + gate[:, kk:kk + 1] * buf[slot, kk].reshape(x1_ref.shape)
    x2 = x1_ref[...] + gt_ref[...] * y
    o_ref[...] = _rms(x2, gfin_ref[...])


def _combine(x1, mod3, y_rows, dest, gates, g_final, tm, tok_offset):
    t, d = x1.shape
    g_cnt, r, _ = mod3.shape
    tiles_per_group = (t // tm) // g_cnt
    full = pl.BlockSpec((tm, d), lambda i, ds: (i, 0))
    grid_spec = pltpu.PrefetchScalarGridSpec(
        num_scalar_prefetch=1,
        grid=(t // tm,),
        in_specs=[full,
                  pl.BlockSpec((None, r, d), lambda i, ds: (i // tiles_per_group, 0, 5)),
                  pl.BlockSpec((tm, LANES), lambda i, ds: (i, 0)),
                  pl.BlockSpec((1, d), lambda i, ds: (0, 0)),
                  pl.BlockSpec(memory_space=pl.ANY)],
        out_specs=full,
        scratch_shapes=[pltpu.VMEM((2, TOP_K, tm // F32_SUBLANES, F32_SUBLANES, d), F32),
                        pltpu.SemaphoreType.DMA((2,))],
    )
    return pl.pallas_call(
        functools.partial(_combine_kernel, tok_offset=tok_offset),
        grid_spec=grid_spec,
        out_shape=jax.ShapeDtypeStruct((t, d), F32),
        compiler_params=_cparams(("arbitrary",), 40),
        name="combine",
    )(dest, x1, mod3, gates, g_final.reshape(1, d), y_rows)


def _route(top_idx):
    n = top_idx.shape[0]
    n_assign = n * TOP_K
    n_sb = n_assign // MOE_ROWS + N_EXPERTS
    e_flat = top_idx.reshape(-1)
    onehot = (e_flat[:, None] == jnp.arange(N_EXPERTS)[None, :]).astype(I32)
    csum = jnp.cumsum(onehot, axis=0)
    counts = csum[-1]
    rank = jnp.take_along_axis(csum, e_flat[:, None], axis=1)[:, 0] - 1
    seg_start = jnp.cumsum(counts) - counts
    src_sorted = (jnp.argsort(e_flat, stable=True) // TOP_K).astype(I32)
    sb_per_e = (counts + MOE_ROWS - 1) // MOE_ROWS
    sb_end = jnp.cumsum(sb_per_e)
    sb_start = sb_end - sb_per_e
    n_used = sb_end[-1]
    dest = (sb_start[e_flat] * MOE_ROWS + rank).astype(I32)
    sb = jnp.arange(n_sb)
    sb_clamped = jnp.minimum(sb, n_used - 1)
    sb_expert = jnp.minimum(jnp.searchsorted(sb_end, sb_clamped, side='right'), N_EXPERTS - 1).astype(I32)
    local = sb_clamped - sb_start[sb_expert]
    sb_rows = jnp.where(sb < n_used, jnp.clip(counts[sb_expert] - local * MOE_ROWS, 0, MOE_ROWS), 0).astype(I32)
    sb_off = (seg_start[sb_expert] + local * MOE_ROWS).astype(I32)
    sb_out = jnp.concatenate([jnp.where(sb < n_used, sb, n_sb), (n_used - 1)[None]]).astype(I32)
    return dest, src_sorted, sb_expert, sb_rows, sb_off, sb_out


def kernel(x_prompt, x_sample, cache_k, cache_v, page_table, c_prompt, c_sample, rel_bias_table, w_ada, b_ada,
           g_mix, w_in, g_sgu, w_sp, b_sp, g_out_attn, g_out_sgu, w_out, g_ffn, w_router, b_router,
           w_gate_up, b_gate_up, w_down, b_down, g_final):
    batch, seq, d = x_prompt.shape
    dec_batch, dec_seq, _ = x_sample.shape
    depth = w_ada.shape[0]
    assert depth == 1 and d == D_MODEL and seq % MOBA_BLOCK == 0 and seq // MOBA_BLOCK <= LANES
    n_p = batch * seq
    n_s = dec_batch * dec_seq
    past_len = page_table.shape[1] * PAGE_SIZE
    assert past_len % MOBA_BLOCK == 0 and dec_seq <= SGU_CHUNK and SGU_CHUNK % dec_seq == 0
    width = N_HEADS * HEAD_DIM
    drop = lambda a: a.reshape(a.shape[1:])

    mod = _adaln(jnp.concatenate([c_prompt, c_sample], axis=0), drop(w_ada), drop(b_ada))
    mod_p = mod[:batch].reshape(batch, 1, 6 * d)
    tm_s = min(n_s, 256)
    assert n_s % tm_s == 0 and tm_s % SGU_CHUNK == 0
    mod_s = jnp.repeat(mod[batch:], dec_seq, axis=0).reshape(n_s // tm_s, tm_s, 6 * d)

    w_in_bf = drop(w_in).astype(BF16)
    w_out_bf = drop(w_out).astype(BF16)
    xp = x_prompt.reshape(n_p, d)
    xs = x_sample.reshape(n_s, d)
    g_sgu_flat = g_sgu.reshape(-1)
    g_mix1, g_ffn1, g_oa, g_os = drop(g_mix), drop(g_ffn), drop(g_out_attn), drop(g_out_sgu)
    w_sp1, b_sp1 = drop(w_sp), drop(b_sp)

    q_p, k_p, v_p, u_p, vs_p = _inproj(xp, mod_p, g_mix1, w_in_bf, g_sgu_flat, 512)
    q_s, k_s, v_s, u_s, vs_s = _inproj(xs, mod_s, g_mix1, w_in_bf, g_sgu_flat, tm_s)

    attn_p = _attn_prompt(q_p, k_p, v_p, _bias_prompt(rel_bias_table), batch, seq)
    qh_rows = dec_seq * N_HEADS
    assert qh_rows <= LANES
    per_head = lambda a: a.reshape(dec_batch, qh_rows, HEAD_DIM)
    pad_new = lambda a: jnp.pad(per_head(a), ((0, 0), (0, LANES - qh_rows), (0, 0)))
    page_rows = PAGE_SIZE * N_HEADS
    attn_s = _attn_sample(per_head(q_s), pad_new(k_s), pad_new(v_s),
                          cache_k.reshape(-1, page_rows, HEAD_DIM), cache_v.reshape(-1, page_rows, HEAD_DIM),
                          page_table, _bias_sample(rel_bias_table, past_len, dec_seq)).reshape(n_s, width)

    tri = jnp.tril(jnp.ones((SGU_CHUNK, SGU_CHUNK), F32))
    reps = SGU_CHUNK // dec_seq
    seq_id = jnp.arange(SGU_CHUNK) // dec_seq
    mask_s = tri * (seq_id[:, None] == seq_id[None, :]).astype(F32)
    wmix_s = jnp.tile(w_sp1[:, :dec_seq, :dec_seq], (1, reps, reps))
    bmix_p = jnp.repeat(b_sp1.T, SGU_WIDTH // N_GROUPS, axis=1)
    bmix_s = jnp.repeat(jnp.tile(b_sp1[:, :dec_seq], (1, reps)).T, SGU_WIDTH // N_GROUPS, axis=1)
    wr_pad = jnp.pad(drop(w_router), ((0, 0), (0, LANES - N_EXPERTS))).astype(BF16)
    br_pad = jnp.pad(drop(b_router), (0, LANES - N_EXPERTS), constant_values=NEG).reshape(1, LANES)

    x1_p, h2_p, ti_p, tg_p = _mixout(attn_p, u_p, vs_p, xp, mod_p, w_sp1, tri, bmix_p, g_oa,
                                     g_os, w_out_bf, g_ffn1, wr_pad, br_pad, 256)
    x1_s, h2_s, ti_s, tg_s = _mixout(attn_s, u_s, vs_s, xs, mod_s, wmix_s, mask_s, bmix_s, g_oa,
                                     g_os, w_out_bf, g_ffn1, wr_pad, br_pad, tm_s)

    h2 = jnp.concatenate([h2_p, h2_s], axis=0)
    top_idx = jnp.concatenate([ti_p, ti_s], axis=0)[:, :TOP_K]
    dest, src_sorted, sb_expert, sb_rows, sb_off, sb_out = _route(top_idx)
    y_rows = _moe(h2, src_sorted, sb_expert, sb_rows, sb_off, sb_out,
                  drop(w_gate_up), drop(b_gate_up), drop(w_down), drop(b_down))

    y_p = _combine(x1_p, mod_p, y_rows, dest, tg_p, g_final, 256, 0)
    y_s = _combine(x1_s, mod_s, y_rows, dest, tg_s, g_final, tm_s, n_p)

    hd = (N_HEADS, HEAD_DIM)
    gd = (N_GROUPS, SGU_WIDTH // N_GROUPS)
    last_chunk = ((seq - 1) // SGU_CHUNK) * SGU_CHUNK
    return (y_p.reshape(batch, seq, d), y_s.reshape(dec_batch, dec_seq, d),
            k_p.reshape(1, batch, seq, *hd), v_p.reshape(1, batch, seq, *hd),
            k_s.reshape(1, dec_batch, dec_seq, *hd), v_s.reshape(1, dec_batch, dec_seq, *hd),
            vs_p.reshape(batch, seq, *gd)[:, last_chunk:][None], vs_s.reshape(1, dec_batch, dec_seq, *gd))
```

```python
import functools
import math

import numpy as np
import jax
import jax.numpy as jnp
from jax import lax
from jax.experimental import pallas as pl
from jax.experimental.pallas import tpu as pltpu

F32 = jnp.float32
BF16 = jnp.bfloat16
I32 = jnp.int32

D_MODEL = 2048
ATTN_WIDTH = 1024
SGU_WIDTH = 1024
HEAD_DIM = 128
N_HEADS = 8
N_GROUPS = 8
MOBA_BLOCK = 256
MOBA_TOPK = 3
SGU_CHUNK = 128
N_BUCKETS = 32
MAX_DISTANCE = 128
N_EXPERTS = 32
TOP_K = 4
SWIGLU_ALPHA = 1.702
SWIGLU_LIMIT = 7.0
RMS_EPS = 1e-5
NEG = -1e30
PICKED = -3e38
PAGE_SIZE = 128
LANES = 128
MASKED_BUCKET = N_BUCKETS

MOE_ROWS = 3072
MOE_CHUNK = 256
MOE_TILE = 256
GATHER_ROWS = 128
GATHER_SLOTS = 6
F32_SUBLANES = 8
BF16_SUBLANES = 16


def _cparams(sem, vmem_mb):
    return pltpu.CompilerParams(dimension_semantics=sem, vmem_limit_bytes=vmem_mb << 20)


def _rms(x, g):
    return x * lax.rsqrt(jnp.mean(x * x, axis=-1, keepdims=True) + RMS_EPS) * g


def _gelu(x):
    return 0.5 * x * (1.0 + lax.erf(x * (2.0 ** -0.5)))


def _adaln_kernel(c_ref, w_ref, b_ref, o_ref):
    c = c_ref[...]
    a = (c * jax.nn.sigmoid(c)).astype(BF16)
    o_ref[...] = jnp.dot(a, w_ref[...].astype(BF16), preferred_element_type=F32) + b_ref[...]


def _adaln(c, w_ada, b_ada):
    r, d = c.shape
    n = w_ada.shape[1]
    tn = 1024
    return pl.pallas_call(
        _adaln_kernel,
        grid=(n // tn,),
        in_specs=[pl.BlockSpec((r, d), lambda i: (0, 0)),
                  pl.BlockSpec((d, tn), lambda i: (0, i)),
                  pl.BlockSpec((1, tn), lambda i: (0, i))],
        out_specs=pl.BlockSpec((r, tn), lambda i: (0, i)),
        out_shape=jax.ShapeDtypeStruct((r, n), F32),
        compiler_params=_cparams(("arbitrary",), 40),
        name="adaln",
    )(c, w_ada, b_ada.reshape(1, n))


def _inproj_kernel(x_ref, sh_ref, sc_ref, gmix_ref, w_ref, gsgu_ref,
                   q_ref, k_ref, v_ref, u_ref, vs_ref, h_sc):
    n = pl.program_id(1)

    @pl.when(n == 0)
    def _():
        h = _rms(x_ref[...], gmix_ref[...]) * (1.0 + sc_ref[...]) + sh_ref[...]
        h_sc[...] = h.astype(BF16)

    p = jnp.dot(h_sc[...], w_ref[...], preferred_element_type=F32)

    @pl.when(n == 0)
    def _():
        q_ref[...] = p

    @pl.when(n == 1)
    def _():
        k_ref[...] = p

    @pl.when(n == 2)
    def _():
        v_ref[...] = p

    @pl.when(n == 3)
    def _():
        u_ref[...] = _gelu(p)

    @pl.when(n == 4)
    def _():
        g = _gelu(p)
        for i in range(N_GROUPS):
            sl = slice(i * LANES, (i + 1) * LANES)
            vs_ref[:, sl] = _rms(g[:, sl], gsgu_ref[:, sl])


def _inproj(x, mod3, g_mix, w_in_bf, g_sgu, tm):
    t, d = x.shape
    g_cnt, r, _ = mod3.shape
    tiles_per_group = (t // tm) // g_cnt
    w = ATTN_WIDTH
    mod_spec = lambda chunk: pl.BlockSpec((None, r, d), lambda i, n: (i // tiles_per_group, 0, chunk))
    out_spec = pl.BlockSpec((tm, w), lambda i, n: (i, 0))
    out_sds = jax.ShapeDtypeStruct((t, w), F32)
    return pl.pallas_call(
        _inproj_kernel,
        grid=(t // tm, 5),
        in_specs=[pl.BlockSpec((tm, d), lambda i, n: (i, 0)),
                  mod_spec(0), mod_spec(1),
                  pl.BlockSpec((1, d), lambda i, n: (0, 0)),
                  pl.BlockSpec((d, w), lambda i, n: (0, n)),
                  pl.BlockSpec((1, w), lambda i, n: (0, 0))],
        out_specs=[out_spec] * 5,
        out_shape=[out_sds] * 5,
        scratch_shapes=[pltpu.VMEM((tm, d), BF16)],
        compiler_params=_cparams(("parallel", "arbitrary"), 56),
        name="inproj",
    )(x, mod3, mod3, g_mix.reshape(1, d), w_in_bf, g_sgu.reshape(1, w))


def _bucket_np(dist):
    dist = np.maximum(dist, 0)
    max_exact = N_BUCKETS // 2
    dist_f = np.maximum(dist, max_exact).astype(np.float32)
    large = max_exact + (np.log(dist_f / np.float32(max_exact)) / np.float32(math.log(MAX_DISTANCE / max_exact))
                         * np.float32(N_BUCKETS - max_exact)).astype(np.int32)
    return np.where(dist < max_exact, dist, np.minimum(large, N_BUCKETS - 1)).astype(np.int32)


def _bias_prompt_kernel(tab_ref, bucket_ref, o_ref):
    h = pl.program_id(0)
    b = bucket_ref[...]
    acc = jnp.zeros(b.shape, F32)
    for kk in range(N_BUCKETS):
        acc = jnp.where(b == kk, tab_ref[kk, h], acc)
    o_ref[0] = jnp.where(b == MASKED_BUCKET, NEG, acc)


def _bias_prompt(table):
    r = np.arange(MOBA_BLOCK)[:, None]
    c = np.arange(MOBA_BLOCK)[None, :]
    own = np.where(c <= r, _bucket_np(r - c), MASKED_BUCKET)
    prev = _bucket_np(MOBA_BLOCK + r - c)
    far = _bucket_np(2 * MOBA_BLOCK + r - c)
    assert (far == far[0, 0]).all()
    buckets = jnp.asarray(np.stack([own, prev, far]).astype(np.int32))
    return pl.pallas_call(
        _bias_prompt_kernel,
        grid=(N_HEADS,),
        in_specs=[pl.BlockSpec(memory_space=pltpu.SMEM),
                  pl.BlockSpec((3, MOBA_BLOCK, MOBA_BLOCK), lambda h: (0, 0, 0))],
        out_specs=pl.BlockSpec((1, 3, MOBA_BLOCK, MOBA_BLOCK), lambda h: (h, 0, 0, 0)),
        out_shape=jax.ShapeDtypeStruct((N_HEADS, 3, MOBA_BLOCK, MOBA_BLOCK), F32),
        compiler_params=_cparams(("arbitrary",), 16),
        name="bias_prompt",
    )(table, buckets)


def _bias_sample_kernel(tabt_ref, bucket_ref, o_ref):
    b = bucket_ref[...]
    acc = jnp.zeros(b.shape, F32)
    for kk in range(N_BUCKETS):
        acc = jnp.where(b == kk, tabt_ref[:, kk:kk + 1], acc)
    o_ref[...] = jnp.where(b == MASKED_BUCKET, NEG, acc)


def _bias_sample(table, past_len, dec_seq):
    rows = dec_seq * N_HEADS
    qi = (np.arange(rows) // N_HEADS)[:, None]
    qh = (np.arange(rows) % N_HEADS)[:, None]
    col = np.arange(past_len * N_HEADS)[None, :]
    past = np.where(col % N_HEADS == qh, _bucket_np(past_len + qi - col // N_HEADS), MASKED_BUCKET)
    col = np.arange(LANES)[None, :]
    own_ok = (col % N_HEADS == qh) & (col // N_HEADS <= qi)
    own = np.where(own_ok, _bucket_np(qi - col // N_HEADS), MASKED_BUCKET)
    buckets = jnp.asarray(np.concatenate([past, own], axis=1).astype(np.int32))
    tab_t = jnp.tile(table, (1, dec_seq)).T
    return pl.pallas_call(
        _bias_sample_kernel,
        out_shape=jax.ShapeDtypeStruct(buckets.shape, F32),
        name="bias_sample",
    )(tab_t, buckets)


def _attn_prompt_kernel(q_ref, k_ref, v_ref, bias_ref, o_ref, ka_sc, vb_sc, km_sc, s_sc):
    blk = MOBA_BLOCK
    seq = k_ref.shape[0]
    nb = seq // blk
    mrows = km_sc.shape[0]
    scale = HEAD_DIM ** -0.5
    nt = (((1,), (1,)), ((), ()))

    k = k_ref[...]
    ka_sc[:, :HEAD_DIM] = k.astype(BF16)
    row_blk = lax.broadcasted_iota(I32, (seq, LANES), 0) // blk
    ka_sc[:, HEAD_DIM:] = (lax.broadcasted_iota(I32, (seq, LANES), 1) == row_blk).astype(BF16)
    vb_sc[...] = v_ref[...].astype(BF16)
    means = [jnp.mean(k[n * blk:(n + 1) * blk, :], axis=0, keepdims=True) for n in range(nb)]
    means.append(jnp.zeros((mrows - nb, HEAD_DIM), F32))
    km_sc[...] = jnp.concatenate(means, axis=0).astype(BF16)
    sub = lax.broadcasted_iota(I32, (mrows, blk), 0).astype(F32)

    def rows(n):
        return slice(n * blk, (n + 1) * blk)

    slot = 0
    for j in range(nb):
        qb = q_ref[rows(j), :].astype(BF16)
        if j > MOBA_TOPK:
            gates = lax.dot_general(km_sc[...], qb, nt, preferred_element_type=F32)
            gates = jnp.where(sub < float(j), gates, NEG)
            allow = jnp.full((mrows, blk), NEG, F32)
            for _ in range(MOBA_TOPK):
                best = jnp.max(gates, axis=0, keepdims=True)
                first = jnp.min(jnp.where(gates == best, sub, float(mrows)), axis=0, keepdims=True)
                hit = sub == first
                allow = jnp.where(hit, 0.0, allow)
                gates = jnp.where(hit, PICKED, gates)
            allow = jnp.concatenate([allow, jnp.full((LANES - mrows, blk), NEG, F32)], axis=0).T
            q_past = jnp.concatenate([qb, allow.astype(BF16)], axis=1)
            k_cols = slice(0, HEAD_DIM + LANES)
        else:
            q_past = qb
            k_cols = slice(0, HEAD_DIM)

        s_own = lax.dot_general(qb, ka_sc[rows(j), :HEAD_DIM], nt, preferred_element_type=F32) * scale
        s_own = s_own + bias_ref[0, 0]
        run = s_own
        base = slot
        for n in range(j):
            s = lax.dot_general(q_past, ka_sc[rows(n), k_cols], nt, preferred_element_type=F32) * scale
            s = s + bias_ref[0, min(j - n, 2)]
            s_sc[base + n] = s
            run = jnp.maximum(run, s)
        slot += j
        m = jnp.max(run, axis=-1, keepdims=True)
        p = jnp.exp(s_own - m)
        psum = p
        acc = jnp.dot(p.astype(BF16), vb_sc[rows(j), :], preferred_element_type=F32)
        for n in range(j):
            p = jnp.exp(s_sc[base + n] - m)
            psum = psum + p
            acc = acc + jnp.dot(p.astype(BF16), vb_sc[rows(n), :], preferred_element_type=F32)
        o_ref[rows(j), :] = acc / jnp.sum(psum, axis=-1, keepdims=True)


def _attn_prompt(q, k, v, bias_tiles, batch, seq):
    nb = seq // MOBA_BLOCK
    spec = pl.BlockSpec((seq, HEAD_DIM), lambda b, h: (b, h))
    return pl.pallas_call(
        _attn_prompt_kernel,
        grid=(batch, N_HEADS),
        in_specs=[spec, spec, spec,
                  pl.BlockSpec((1, 3, MOBA_BLOCK, MOBA_BLOCK), lambda b, h: (h, 0, 0, 0))],
        out_specs=spec,
        out_shape=jax.ShapeDtypeStruct(q.shape, F32),
        scratch_shapes=[pltpu.VMEM((seq, HEAD_DIM + LANES), BF16), pltpu.VMEM((seq, HEAD_DIM), BF16),
                        pltpu.VMEM((pl.cdiv(nb, BF16_SUBLANES) * BF16_SUBLANES, HEAD_DIM), BF16),
                        pltpu.VMEM((max(nb * (nb - 1) // 2, 1), MOBA_BLOCK, MOBA_BLOCK), F32)],
        compiler_params=_cparams(("parallel", "parallel"), 40),
        name="attn_prompt",
    )(q, k, v, bias_tiles)


PAGES_PER_STEP = 16


def _attn_sample_kernel(pt_ref, q_ref, kn_ref, vn_ref, bias_ref, *refs, n_pages):
    del pt_ref
    pps = PAGES_PER_STEP
    k_refs = refs[:pps]
    v_refs = refs[pps:2 * pps]
    o_ref, s_sc, acc_sc, den_sc = refs[2 * pps:]
    s = pl.program_id(1)
    k_steps = n_pages // pps
    rows = q_ref.shape[0]
    cols = PAGE_SIZE * N_HEADS
    n_blocks = n_pages * PAGE_SIZE // MOBA_BLOCK
    ppb = MOBA_BLOCK // PAGE_SIZE
    scale = HEAD_DIM ** -0.5
    nt = (((1,), (1,)), ((), ()))

    @pl.when(s < k_steps)
    def _():
        qb = q_ref[...].astype(BF16)
        for pg in range(pps):
            kp = k_refs[pg][...].astype(BF16)
            s_sc[s * pps + pg] = lax.dot_general(qb, kp, nt, preferred_element_type=F32)

    @pl.when(s == k_steps - 1)
    def _():
        same_head = (lax.broadcasted_iota(I32, (rows, cols), 1) % N_HEADS
                     == lax.broadcasted_iota(I32, (rows, cols), 0) % N_HEADS)
        gates = []
        for n in range(n_blocks):
            tot = s_sc[n * ppb]
            for t in range(1, ppb):
                tot = tot + s_sc[n * ppb + t]
            gates.append(jnp.sum(jnp.where(same_head, tot, 0.0), axis=-1, keepdims=True))
        allow = []
        for n in range(n_blocks):
            ahead = jnp.zeros((rows, 1), F32)
            for mm in range(n_blocks):
                if mm == n:
                    continue
                before = (gates[mm] >= gates[n]) if mm < n else (gates[mm] > gates[n])
                ahead = ahead + before.astype(F32)
            allow.append(jnp.where(ahead < MOBA_TOPK, 0.0, NEG))
        qb = q_ref[...].astype(BF16)
        past_cols = n_pages * cols
        own = lax.dot_general(qb, kn_ref[...].astype(BF16), nt, preferred_element_type=F32) * scale
        own = own + bias_ref[:, past_cols:]
        run = jnp.full((rows, cols), NEG, F32)
        for t in range(n_pages):
            lg = s_sc[t] * scale + bias_ref[:, t * cols:(t + 1) * cols] + allow[t // ppb]
            s_sc[t] = lg
            run = jnp.maximum(run, lg)
        m = jnp.maximum(jnp.max(own, axis=-1, keepdims=True), jnp.max(run, axis=-1, keepdims=True))
        p_own = jnp.exp(own - m)
        psum = jnp.zeros((rows, cols), F32)
        for t in range(n_pages):
            p = jnp.exp(s_sc[t] - m)
            s_sc[t] = p
            psum = psum + p
        den = jnp.sum(p_own, axis=-1, keepdims=True) + jnp.sum(psum, axis=-1, keepdims=True)
        den_sc[...] = jnp.broadcast_to(den, den_sc.shape)
        acc_sc[...] = jnp.dot(p_own.astype(BF16), vn_ref[...].astype(BF16), preferred_element_type=F32)

    @pl.when(s >= k_steps)
    def _():
        acc = acc_sc[...]
        for pg in range(pps):
            p = s_sc[(s - k_steps) * pps + pg].astype(BF16)
            acc = acc + jnp.dot(p, v_refs[pg][...].astype(BF16), preferred_element_type=F32)
        acc_sc[...] = acc

    @pl.when(s == 2 * k_steps - 1)
    def _():
        o_ref[...] = acc_sc[...] / den_sc[...]


def _attn_sample(q, k_new, v_new, cache_k, cache_v, page_table, bias_s):
    b, rows, hd = q.shape
    n_pages = page_table.shape[1]
    pps = PAGES_PER_STEP
    assert n_pages % pps == 0
    k_steps = n_pages // pps
    cols = PAGE_SIZE * N_HEADS
    q_spec = pl.BlockSpec((None, rows, hd), lambda i, s, pt: (i, 0, 0))
    new_spec = pl.BlockSpec((None, LANES, hd), lambda i, s, pt: (i, 0, 0))

    def k_spec(pg):
        return pl.BlockSpec((None, cols, hd),
                            lambda i, s, pt: (pt[i * n_pages + jnp.minimum(s, k_steps - 1) * pps + pg], 0, 0))

    def v_spec(pg):
        def index(i, s, pt):
            seq = jnp.where(s >= k_steps, i, jnp.maximum(i - 1, 0))
            group = jnp.where(s >= k_steps, s - k_steps, k_steps - 1)
            return (pt[seq * n_pages + group * pps + pg], 0, 0)
        return pl.BlockSpec((None, cols, hd), index)

    grid_spec = pltpu.PrefetchScalarGridSpec(
        num_scalar_prefetch=1,
        grid=(b, 2 * k_steps),
        in_specs=[q_spec, new_spec, new_spec,
                  pl.BlockSpec(bias_s.shape, lambda i, s, pt: (0, 0))]
                 + [k_spec(pg) for pg in range(pps)] + [v_spec(pg) for pg in range(pps)],
        out_specs=q_spec,
        scratch_shapes=[pltpu.VMEM((n_pages, rows, cols), F32),
                        pltpu.VMEM((rows, hd), F32),
                        pltpu.VMEM((rows, hd), F32)],
    )
    return pl.pallas_call(
        functools.partial(_attn_sample_kernel, n_pages=n_pages),
        grid_spec=grid_spec,
        out_shape=jax.ShapeDtypeStruct(q.shape, F32),
        compiler_params=_cparams(("parallel", "arbitrary"), 48),
        name="attn_sample",
    )(page_table.reshape(-1), q, k_new, v_new, bias_s, *([cache_k] * pps), *([cache_v] * pps))


def _mixout_kernel(attn_ref, u_ref, vs_ref, x_ref, gt_ref, sc_ref, sh_ref, wmix_ref, mask_ref, bmix_ref,
                   ga_ref, gs_ref, wout_ref, gffn_ref, wr_ref, br_ref,
                   x1_ref, h2_ref, topi_ref, topg_ref, sgu_sc):
    tm = x_ref.shape[0]
    for g in range(N_GROUPS):
        cols = slice(g * LANES, (g + 1) * LANES)
        wg = (wmix_ref[g] * mask_ref[...]).astype(BF16)
        for c in range(tm // SGU_CHUNK):
            rws = slice(c * SGU_CHUNK, (c + 1) * SGU_CHUNK)
            mixed = jnp.dot(wg, vs_ref[rws, cols].astype(BF16), preferred_element_type=F32) + bmix_ref[:, cols]
            sgu_sc[rws, cols] = u_ref[rws, cols] * mixed
    an = _rms(attn_ref[...], ga_ref[...]).astype(BF16)
    sn = _rms(sgu_sc[...], gs_ref[...]).astype(BF16)
    out = (jnp.dot(an, wout_ref[:ATTN_WIDTH, :], preferred_element_type=F32)
           + jnp.dot(sn, wout_ref[ATTN_WIDTH:, :], preferred_element_type=F32))
    x1 = x_ref[...] + gt_ref[...] * out
    x1_ref[...] = x1
    h2 = _rms(x1, gffn_ref[...]) * (1.0 + sc_ref[...]) + sh_ref[...]
    h2_ref[...] = h2
    logits = jnp.dot(h2.astype(BF16), wr_ref[...], preferred_element_type=F32) + br_ref[...]
    lane = lax.broadcasted_iota(I32, logits.shape, 1).astype(F32)
    idx_out = jnp.zeros(logits.shape, F32)
    val_out = jnp.zeros(logits.shape, F32)
    top = None
    den = None
    for r in range(TOP_K):
        best = jnp.max(logits, axis=-1, keepdims=True)
        first = jnp.min(jnp.where(logits == best, lane, float(LANES)), axis=-1, keepdims=True)
        if r == 0:
            top = best
        e = jnp.exp(best - top)
        den = e if r == 0 else den + e
        idx_out = jnp.where(lane == float(r), first, idx_out)
        val_out = jnp.where(lane == float(r), e, val_out)
        logits = jnp.where(lane == first, PICKED, logits)
    topi_ref[...] = idx_out.astype(I32)
    topg_ref[...] = val_out / den


def _mixout(attn, u, vs, x, mod3, wmix, mixmask, bmix, g_out_attn, g_out_sgu, w_out_bf, g_ffn, wr_pad, br_pad, tm):
    t, d = x.shape
    g_cnt, r, _ = mod3.shape
    tiles_per_group = (t // tm) // g_cnt
    w = ATTN_WIDTH
    mod_spec = lambda chunk: pl.BlockSpec((None, r, d), lambda i: (i // tiles_per_group, 0, chunk))
    half = pl.BlockSpec((tm, w), lambda i: (i, 0))
    full = pl.BlockSpec((tm, d), lambda i: (i, 0))
    small = pl.BlockSpec((tm, LANES), lambda i: (i, 0))
    const = lambda shape: pl.BlockSpec(shape, lambda i: (0,) * len(shape))
    return pl.pallas_call(
        _mixout_kernel,
        grid=(t // tm,),
        in_specs=[half, half, half, full, mod_spec(2), mod_spec(4), mod_spec(3),
                  const(wmix.shape), const(mixmask.shape), const(bmix.shape),
                  const((1, w)), const((1, w)), const((d, d)), const((1, d)),
                  const((d, LANES)), const((1, LANES))],
        out_specs=[full, full, small, small],
        out_shape=[jax.ShapeDtypeStruct((t, d), F32), jax.ShapeDtypeStruct((t, d), F32),
                   jax.ShapeDtypeStruct((t, LANES), I32), jax.ShapeDtypeStruct((t, LANES), F32)],
        scratch_shapes=[pltpu.VMEM((tm, w), F32)],
        compiler_params=_cparams(("parallel",), 48),
        name="mixout",
    )(attn, u, vs, x, mod3, mod3, mod3, wmix, mixmask, bmix, g_out_attn.reshape(1, w), g_out_sgu.reshape(1, w),
      w_out_bf, g_ffn.reshape(1, d), wr_pad, br_pad)


def _moe_kernel(src_ref, exp_ref, nv_ref, off_ref, oblk_ref, h_hbm, wg_ref, wu_ref, bg_ref, bu_ref, wd_ref, bd_ref,
                y_ref, x_sc, act_sc, stage_sc, sem):
    del exp_ref, oblk_ref
    sb = pl.program_id(0)
    t = pl.program_id(1)
    n_f = act_sc.shape[0]
    nv = nv_ref[sb]
    n_chunks = (nv + MOE_CHUNK - 1) // MOE_CHUNK

    @pl.when((t == 0) & (nv > 0))
    def _():
        off = off_ref[sb]
        last = off + nv - 1
        n_batches = n_chunks * (MOE_CHUNK // GATHER_ROWS)

        def row_copy(bt, slot, i, u):
            tok = src_ref[jnp.minimum(off + bt * GATHER_ROWS + i * F32_SUBLANES + u, last)]
            return pltpu.make_async_copy(h_hbm.at[pl.ds(tok, 1)], stage_sc.at[slot, i, pl.ds(u, 1)], sem.at[slot])

        def issue(bt, slot):
            def group(i, c):
                for u in range(F32_SUBLANES):
                    row_copy(bt, slot, i, u).start()
                return c
            lax.fori_loop(0, GATHER_ROWS // F32_SUBLANES, group, 0)

        def drain(bt, slot):
            def group(i, c):
                for u in range(F32_SUBLANES):
                    row_copy(bt, slot, i, u).wait()
                return c
            lax.fori_loop(0, GATHER_ROWS // F32_SUBLANES, group, 0)
            rows = stage_sc[slot].reshape(GATHER_ROWS, x_sc.shape[1])
            x_sc[pl.ds(pl.multiple_of(bt * GATHER_ROWS, GATHER_ROWS), GATHER_ROWS), :] = rows.astype(BF16)

        n_slots = stage_sc.shape[0]
        ahead = n_slots - 1
        for bt in range(ahead):
            @pl.when(bt < n_batches)
            def _():
                issue(bt, bt)

        def body(bt, c):
            @pl.when(bt + ahead < n_batches)
            def _():
                issue(bt + ahead, (bt + ahead) % n_slots)

            drain(bt, bt % n_slots)
            return c

        lax.fori_loop(0, n_batches, body, 0)

    def cascade(fn):
        big = 8 * MOE_CHUNK
        n_big = n_chunks // 8

        def body(i, c):
            fn(pl.multiple_of(i * big, big), big)
            return c

        lax.fori_loop(0, n_big, body, 0)
        rem = n_chunks - n_big * 8
        base = n_big * big
        for part in (4, 2, 1):
            done = (rem // (2 * part)) * (2 * part)

            @pl.when((rem // part) % 2 == 1)
            def _(part=part, done=done):
                fn(pl.multiple_of(base + done * MOE_CHUNK, MOE_CHUNK), part * MOE_CHUNK)

    @pl.when(t < n_f)
    def _():
        bg = bg_ref[...]
        bu = bu_ref[...]

        def up_proj(start, size):
            xb = x_sc[pl.ds(start, size), :]
            g = jnp.dot(xb, wg_ref[...].astype(BF16), preferred_element_type=F32) + bg
            up = jnp.dot(xb, wu_ref[...].astype(BF16), preferred_element_type=F32) + bu
            g = jnp.minimum(g, SWIGLU_LIMIT)
            up = jnp.clip(up, -SWIGLU_LIMIT, SWIGLU_LIMIT)
            act = g * jax.nn.sigmoid(SWIGLU_ALPHA * g) * (up + 1.0)
            act_sc[t, pl.ds(start, size), :] = act.astype(BF16)

        cascade(up_proj)

    @pl.when(t >= n_f)
    def _():
        bd = bd_ref[...]

        def down_proj(start, size):
            acc = jnp.zeros((size, MOE_TILE), F32) + bd
            for f in range(n_f):
                wd = wd_ref[f * MOE_TILE:(f + 1) * MOE_TILE, :].astype(BF16)
                acc = acc + jnp.dot(act_sc[f, pl.ds(start, size), :], wd, preferred_element_type=F32)
            y_ref[pl.ds(start, size), :] = acc

        cascade(down_proj)


def _moe(h2, src_sorted, sb_expert, sb_rows, sb_off, sb_out, w_gate_up, b_gate_up, w_down, b_down):
    _, d = h2.shape
    n_sb = sb_expert.shape[0]
    e, f_dim = w_down.shape[0], w_down.shape[1]
    n_f = f_dim // MOE_TILE
    n_o = d // MOE_TILE

    def f_tile(s, t, nv):
        return jnp.where(nv[s] > 0, jnp.minimum(t, n_f - 1), n_f - 1)

    def o_tile(s, t, nv):
        return jnp.where(nv[s] > 0, jnp.maximum(t - n_f, 0), n_o - 1)

    grid_spec = pltpu.PrefetchScalarGridSpec(
        num_scalar_prefetch=5,
        grid=(n_sb, n_f + n_o),
        in_specs=[
            pl.BlockSpec(memory_space=pl.ANY),
            pl.BlockSpec((None, d, MOE_TILE), lambda s, t, sr, ex, nv, of, ob: (ex[s], 0, f_tile(s, t, nv))),
            pl.BlockSpec((None, d, MOE_TILE), lambda s, t, sr, ex, nv, of, ob: (ex[s], 0, n_f + f_tile(s, t, nv))),
            pl.BlockSpec((None, 1, MOE_TILE), lambda s, t, sr, ex, nv, of, ob: (ex[s], 0, f_tile(s, t, nv))),
            pl.BlockSpec((None, 1, MOE_TILE), lambda s, t, sr, ex, nv, of, ob: (ex[s], 0, n_f + f_tile(s, t, nv))),
            pl.BlockSpec((None, f_dim, MOE_TILE), lambda s, t, sr, ex, nv, of, ob: (ex[s], 0, o_tile(s, t, nv))),
            pl.BlockSpec((None, 1, MOE_TILE), lambda s, t, sr, ex, nv, of, ob: (ex[s], 0, o_tile(s, t, nv))),
        ],
        out_specs=pl.BlockSpec(
            (MOE_ROWS, MOE_TILE),
            lambda s, t, sr, ex, nv, of, ob: (ob[s], jnp.where(nv[s] > 0, jnp.maximum(t - n_f, 0), 0))),
        scratch_shapes=[pltpu.VMEM((MOE_ROWS, d), BF16),
                        pltpu.VMEM((n_f, MOE_ROWS, MOE_TILE), BF16),
                        pltpu.VMEM((GATHER_SLOTS, GATHER_ROWS // F32_SUBLANES, F32_SUBLANES, d), F32),
                        pltpu.SemaphoreType.DMA((GATHER_SLOTS,))],
    )
    return pl.pallas_call(
        _moe_kernel,
        grid_spec=grid_spec,
        out_shape=jax.ShapeDtypeStruct(((n_sb + 1) * MOE_ROWS, d), F32),
        compiler_params=_cparams(("arbitrary", "arbitrary"), 58),
        name="moe",
    )(src_sorted, sb_expert, sb_rows, sb_off, sb_out, h2, w_gate_up, w_gate_up,
      b_gate_up.reshape(e, 1, -1), b_gate_up.reshape(e, 1, -1), w_down, b_down.reshape(e, 1, -1))


def _combine_kernel(dest_ref, x1_ref, gt_ref, gate_ref, gfin_ref, y_hbm, o_ref, buf, sem, *, tok_offset):
    i = pl.program_id(0)
    n_tiles = pl.num_programs(0)
    tm = x1_ref.shape[0]

    groups = tm // F32_SUBLANES

    def row_copy(tile, slot, g, u, kk):
        row = dest_ref[(tok_offset + tile * tm + g * F32_SUBLANES + u) * TOP_K + kk]
        return pltpu.make_async_copy(y_hbm.at[pl.ds(row, 1)], buf.at[slot, kk, g, pl.ds(u, 1)], sem.at[slot])

    def issue(tile, slot):
        def group(g, c):
            for u in range(F32_SUBLANES):
                for kk in range(TOP_K):
                    row_copy(tile, slot, g, u, kk).start()
            return c
        lax.fori_loop(0, groups, group, 0)

    @pl.when(i == 0)
    def _():
        issue(0, 0)

    @pl.when(i + 1 < n_tiles)
    def _():
        issue(i + 1, (i + 1) % 2)

    slot = i % 2

    def wait_group(g, c):
        for u in range(F32_SUBLANES):
            for kk in range(TOP_K):
                row_copy(i, slot, g, u, kk).wait()
        return c

    lax.fori_loop(0, groups, wait_group, 0)
    gate = gate_ref[...]
    y = jnp.zeros(x1_ref.shape, F32)
    for kk in range(TOP_K):
        y = y + gate[:, kk:kk + 1] * buf[slot, kk].reshape(x1_ref.shape)
    x2 = x1_ref[...] + gt_ref[...] * y
    o_ref[...] = _rms(x2, gfin_ref[...])


def _combine(x1, mod3, y_rows, dest, gates, g_final, tm, tok_offset):
    t, d = x1.shape
    g_cnt, r, _ = mod3.shape
    tiles_per_group = (t // tm) // g_cnt
    full = pl.BlockSpec((tm, d), lambda i, ds: (i, 0))
    grid_spec = pltpu.PrefetchScalarGridSpec(
        num_scalar_prefetch=1,
        grid=(t // tm,),
        in_specs=[full,
                  pl.BlockSpec((None, r, d), lambda i, ds: (i // tiles_per_group, 0, 5)),
                  pl.BlockSpec((tm, LANES), lambda i, ds: (i, 0)),
                  pl.BlockSpec((1, d), lambda i, ds: (0, 0)),
                  pl.BlockSpec(memory_space=pl.ANY)],
        out_specs=full,
        scratch_shapes=[pltpu.VMEM((2, TOP_K, tm // F32_SUBLANES, F32_SUBLANES, d), F32),
                        pltpu.SemaphoreType.DMA((2,))],
    )
    return pl.pallas_call(
        functools.partial(_combine_kernel, tok_offset=tok_offset),
        grid_spec=grid_spec,
        out_shape=jax.ShapeDtypeStruct((t, d), F32),
        compiler_params=_cparams(("arbitrary",), 40),
        name="combine",
    )(dest, x1, mod3, gates, g_final.reshape(1, d), y_rows)


def _route(top_idx):
    n = top_idx.shape[0]
    n_assign = n * TOP_K
    n_sb = n_assign // MOE_ROWS + N_EXPERTS
    e_flat = top_idx.reshape(-1)
    onehot = (e_flat[:, None] == jnp.arange(N_EXPERTS)[None, :]).astype(I32)
    csum = jnp.cumsum(onehot, axis=0)
    counts = csum[-1]
    rank = jnp.take_along_axis(csum, e_flat[:, None], axis=1)[:, 0] - 1
    seg_start = jnp.cumsum(counts) - counts
    src_sorted = (jnp.argsort(e_flat, stable=True) // TOP_K).astype(I32)
    sb_per_e = (counts + MOE_ROWS - 1) // MOE_ROWS
    sb_end = jnp.cumsum(sb_per_e)
    sb_start = sb_end - sb_per_e
    n_used = sb_end[-1]
    dest = (sb_start[e_flat] * MOE_ROWS + rank).astype(I32)
    sb = jnp.arange(n_sb)
    sb_clamped = jnp.minimum(sb, n_used - 1)
    sb_expert = jnp.minimum(jnp.searchsorted(sb_end, sb_clamped, side='right'), N_EXPERTS - 1).astype(I32)
    local = sb_clamped - sb_start[sb_expert]
    sb_rows = jnp.where(sb < n_used, jnp.clip(counts[sb_expert] - local * MOE_ROWS, 0, MOE_ROWS), 0).astype(I32)
    sb_off = (seg_start[sb_expert] + local * MOE_ROWS).astype(I32)
    sb_out = jnp.concatenate([jnp.where(sb < n_used, sb, n_sb), (n_used - 1)[None]]).astype(I32)
    return dest, src_sorted, sb_expert, sb_rows, sb_off, sb_out


def kernel(x_prompt, x_sample, cache_k, cache_v, page_table, c_prompt, c_sample, rel_bias_table, w_ada, b_ada,
           g_mix, w_in, g_sgu, w_sp, b_sp, g_out_attn, g_out_sgu, w_out, g_ffn, w_router, b_router,
           w_gate_up, b_gate_up, w_down, b_down, g_final):
    batch, seq, d = x_prompt.shape
    dec_batch, dec_seq, _ = x_sample.shape
    depth = w_ada.shape[0]
    assert depth == 1 and d == D_MODEL and seq % MOBA_BLOCK == 0 and seq // MOBA_BLOCK <= LANES
    n_p = batch * seq
    n_s = dec_batch * dec_seq
    past_len = page_table.shape[1] * PAGE_SIZE
    assert past_len % MOBA_BLOCK == 0 and dec_seq <= SGU_CHUNK and SGU_CHUNK % dec_seq == 0
    width = N_HEADS * HEAD_DIM
    drop = lambda a: a.reshape(a.shape[1:])

    mod = _adaln(jnp.concatenate([c_prompt, c_sample], axis=0), drop(w_ada), drop(b_ada))
    mod_p = mod[:batch].reshape(batch, 1, 6 * d)
    tm_s = min(n_s, 256)
    assert n_s % tm_s == 0 and tm_s % SGU_CHUNK == 0
    mod_s = jnp.repeat(mod[batch:], dec_seq, axis=0).reshape(n_s // tm_s, tm_s, 6 * d)

    w_in_bf = drop(w_in).astype(BF16)
    w_out_bf = drop(w_out).astype(BF16)
    xp = x_prompt.reshape(n_p, d)
    xs = x_sample.reshape(n_s, d)
    g_sgu_flat = g_sgu.reshape(-1)
    g_mix1, g_ffn1, g_oa, g_os = drop(g_mix), drop(g_ffn), drop(g_out_attn), drop(g_out_sgu)
    w_sp1, b_sp1 = drop(w_sp), drop(b_sp)

    q_p, k_p, v_p, u_p, vs_p = _inproj(xp, mod_p, g_mix1, w_in_bf, g_sgu_flat, 512)
    q_s, k_s, v_s, u_s, vs_s = _inproj(xs, mod_s, g_mix1, w_in_bf, g_sgu_flat, tm_s)

    attn_p = _attn_prompt(q_p, k_p, v_p, _bias_prompt(rel_bias_table), batch, seq)
    qh_rows = dec_seq * N_HEADS
    assert qh_rows <= LANES
    per_head = lambda a: a.reshape(dec_batch, qh_rows, HEAD_DIM)
    pad_new = lambda a: jnp.pad(per_head(a), ((0, 0), (0, LANES - qh_rows), (0, 0)))
    page_rows = PAGE_SIZE * N_HEADS
    attn_s = _attn_sample(per_head(q_s), pad_new(k_s), pad_new(v_s),
                          cache_k.reshape(-1, page_rows, HEAD_DIM), cache_v.reshape(-1, page_rows, HEAD_DIM),
                          page_table, _bias_sample(rel_bias_table, past_len, dec_seq)).reshape(n_s, width)

    tri = jnp.tril(jnp.ones((SGU_CHUNK, SGU_CHUNK), F32))
    reps = SGU_CHUNK // dec_seq
    seq_id = jnp.arange(SGU_CHUNK) // dec_seq
    mask_s = tri * (seq_id[:, None] == seq_id[None, :]).astype(F32)
    wmix_s = jnp.tile(w_sp1[:, :dec_seq, :dec_seq], (1, reps, reps))
    bmix_p = jnp.repeat(b_sp1.T, SGU_WIDTH // N_GROUPS, axis=1)
    bmix_s = jnp.repeat(jnp.tile(b_sp1[:, :dec_seq], (1, reps)).T, SGU_WIDTH // N_GROUPS, axis=1)
    wr_pad = jnp.pad(drop(w_router), ((0, 0), (0, LANES - N_EXPERTS))).astype(BF16)
    br_pad = jnp.pad(drop(b_router), (0, LANES - N_EXPERTS), constant_values=NEG).reshape(1, LANES)

    x1_p, h2_p, ti_p, tg_p = _mixout(attn_p, u_p, vs_p, xp, mod_p, w_sp1, tri, bmix_p, g_oa,
                                     g_os, w_out_bf, g_ffn1, wr_pad, br_pad, 256)
    x1_s, h2_s, ti_s, tg_s = _mixout(attn_s, u_s, vs_s, xs, mod_s, wmix_s, mask_s, bmix_s, g_oa,
                                     g_os, w_out_bf, g_ffn1, wr_pad, br_pad, tm_s)

    h2 = jnp.concatenate([h2_p, h2_s], axis=0)
    top_idx = jnp.concatenate([ti_p, ti_s], axis=0)[:, :TOP_K]
    dest, src_sorted, sb_expert, sb_rows, sb_off, sb_out = _route(top_idx)
    y_rows = _moe(h2, src_sorted, sb_expert, sb_rows, sb_off, sb_out,
                  drop(w_gate_up), drop(b_gate_up), drop(w_down), drop(b_down))

    y_p = _combine(x1_p, mod_p, y_rows, dest, tg_p, g_final, 256, 0)
    y_s = _combine(x1_s, mod_s, y_rows, dest, tg_s, g_final, tm_s, n_p)

    hd = (N_HEADS, HEAD_DIM)
    gd = (N_GROUPS, SGU_WIDTH // N_GROUPS)
    last_chunk = ((seq - 1) // SGU_CHUNK) * SGU_CHUNK
    return (y_p.reshape(batch, seq, d), y_s.reshape(dec_batch, dec_seq, d),
            k_p.reshape(1, batch, seq, *hd), v_p.reshape(1, batch, seq, *hd),
            k_s.reshape(1, dec_batch, dec_seq, *hd), v_s.reshape(1, dec_batch, dec_seq, *hd),
            vs_p.reshape(batch, seq, *gd)[:, last_chunk:][None], vs_s.reshape(1, dec_batch, dec_seq, *gd))
```

```python
import functools
import math

import numpy as np
import jax
import jax.numpy as jnp
from jax import lax
from jax.experimental import pallas as pl
from jax.experimental.pallas import tpu as pltpu

F32 = jnp.float32
BF16 = jnp.bfloat16
I32 = jnp.int32

D_MODEL = 2048
ATTN_WIDTH = 1024
SGU_WIDTH = 1024
HEAD_DIM = 128
N_HEADS = 8
N_GROUPS = 8
MOBA_BLOCK = 256
MOBA_TOPK = 3
SGU_CHUNK = 128
N_BUCKETS = 32
MAX_DISTANCE = 128
N_EXPERTS = 32
TOP_K = 4
SWIGLU_ALPHA = 1.702
SWIGLU_LIMIT = 7.0
RMS_EPS = 1e-5
NEG = -1e30
PICKED = -3e38
PAGE_SIZE = 128
LANES = 128
MASKED_BUCKET = N_BUCKETS

MOE_ROWS = 3072
MOE_CHUNK = 256
MOE_TILE = 256
GATHER_ROWS = 128
GATHER_SLOTS = 6
F32_SUBLANES = 8
BF16_SUBLANES = 16


def _cparams(sem, vmem_mb):
    return pltpu.CompilerParams(dimension_semantics=sem, vmem_limit_bytes=vmem_mb << 20)


def _rms(x, g):
    return x * lax.rsqrt(jnp.mean(x * x, axis=-1, keepdims=True) + RMS_EPS) * g


def _gelu(x):
    return 0.5 * x * (1.0 + lax.erf(x * (2.0 ** -0.5)))


def _adaln_kernel(c_ref, w_ref, b_ref, o_ref):
    c = c_ref[...]
    a = (c * jax.nn.sigmoid(c)).astype(BF16)
    o_ref[...] = jnp.dot(a, w_ref[...].astype(BF16), preferred_element_type=F32) + b_ref[...]


def _adaln(c, w_ada, b_ada):
    r, d = c.shape
    n = w_ada.shape[1]
    tn = 1024
    return pl.pallas_call(
        _adaln_kernel,
        grid=(n // tn,),
        in_specs=[pl.BlockSpec((r, d), lambda i: (0, 0)),
                  pl.BlockSpec((d, tn), lambda i: (0, i)),
                  pl.BlockSpec((1, tn), lambda i: (0, i))],
        out_specs=pl.BlockSpec((r, tn), lambda i: (0, i)),
        out_shape=jax.ShapeDtypeStruct((r, n), F32),
        compiler_params=_cparams(("arbitrary",), 40),
        name="adaln",
    )(c, w_ada, b_ada.reshape(1, n))


def _inproj_kernel(x_ref, sh_ref, sc_ref, gmix_ref, w_ref, gsgu_ref,
                   q_ref, k_ref, v_ref, u_ref, vs_ref, h_sc):
    n = pl.program_id(1)

    @pl.when(n == 0)
    def _():
        h = _rms(x_ref[...], gmix_ref[...]) * (1.0 + sc_ref[...]) + sh_ref[...]
        h_sc[...] = h.astype(BF16)

    def proj(cols):
        return jnp.dot(h_sc[...], w_ref[:, cols], preferred_element_type=F32)

    for idx, out_ref in enumerate((q_ref, k_ref, v_ref)):
        @pl.when(n == idx)
        def _(out_ref=out_ref):
            out_ref[...] = proj(slice(None))

    slab = 2 * LANES

    @pl.when(n == 3)
    def _():
        for c in range(0, ATTN_WIDTH, slab):
            u_ref[:, c:c + slab] = _gelu(proj(slice(c, c + slab)))

    @pl.when(n == 4)
    def _():
        for c in range(0, ATTN_WIDTH, slab):
            g = _gelu(proj(slice(c, c + slab)))
            for i in range(slab // LANES):
                sl = slice(c + i * LANES, c + (i + 1) * LANES)
                vs_ref[:, sl] = _rms(g[:, i * LANES:(i + 1) * LANES], gsgu_ref[:, sl])


def _inproj(x, mod3, g_mix, w_in_bf, g_sgu, tm):
    t, d = x.shape
    g_cnt, r, _ = mod3.shape
    tiles_per_group = (t // tm) // g_cnt
    w = ATTN_WIDTH
    mod_spec = lambda chunk: pl.BlockSpec((None, r, d), lambda i, n: (i // tiles_per_group, 0, chunk))
    out_spec = pl.BlockSpec((tm, w), lambda i, n: (i, 0))
    out_sds = jax.ShapeDtypeStruct((t, w), F32)
    return pl.pallas_call(
        _inproj_kernel,
        grid=(t // tm, 5),
        in_specs=[pl.BlockSpec((tm, d), lambda i, n: (i, 0)),
                  mod_spec(0), mod_spec(1),
                  pl.BlockSpec((1, d), lambda i, n: (0, 0)),
                  pl.BlockSpec((d, w), lambda i, n: (0, n)),
                  pl.BlockSpec((1, w), lambda i, n: (0, 0))],
        out_specs=[out_spec] * 5,
        out_shape=[out_sds] * 5,
        scratch_shapes=[pltpu.VMEM((tm, d), BF16)],
        compiler_params=_cparams(("parallel", "arbitrary"), 56),
        name="inproj",
    )(x, mod3, mod3, g_mix.reshape(1, d), w_in_bf, g_sgu.reshape(1, w))


def _bucket_np(dist):
    dist = np.maximum(dist, 0)
    max_exact = N_BUCKETS // 2
    dist_f = np.maximum(dist, max_exact).astype(np.float32)
    large = max_exact + (np.log(dist_f / np.float32(max_exact)) / np.float32(math.log(MAX_DISTANCE / max_exact))
                         * np.float32(N_BUCKETS - max_exact)).astype(np.int32)
    return np.where(dist < max_exact, dist, np.minimum(large, N_BUCKETS - 1)).astype(np.int32)


def _bias_prompt_kernel(tab_ref, bucket_ref, o_ref):
    h = pl.program_id(0)
    b = bucket_ref[...]
    acc = jnp.zeros(b.shape, F32)
    for kk in range(N_BUCKETS):
        acc = jnp.where(b == kk, tab_ref[kk, h], acc)
    o_ref[0] = jnp.where(b == MASKED_BUCKET, NEG, acc)


def _bias_prompt(table):
    r = np.arange(MOBA_BLOCK)[:, None]
    c = np.arange(MOBA_BLOCK)[None, :]
    own = np.where(c <= r, _bucket_np(r - c), MASKED_BUCKET)
    prev = _bucket_np(MOBA_BLOCK + r - c)
    far = _bucket_np(2 * MOBA_BLOCK + r - c)
    assert (far == far[0, 0]).all()
    buckets = jnp.asarray(np.stack([own, prev, far]).astype(np.int32))
    return pl.pallas_call(
        _bias_prompt_kernel,
        grid=(N_HEADS,),
        in_specs=[pl.BlockSpec(memory_space=pltpu.SMEM),
                  pl.BlockSpec((3, MOBA_BLOCK, MOBA_BLOCK), lambda h: (0, 0, 0))],
        out_specs=pl.BlockSpec((1, 3, MOBA_BLOCK, MOBA_BLOCK), lambda h: (h, 0, 0, 0)),
        out_shape=jax.ShapeDtypeStruct((N_HEADS, 3, MOBA_BLOCK, MOBA_BLOCK), F32),
        compiler_params=_cparams(("arbitrary",), 16),
        name="bias_prompt",
    )(table, buckets)


def _bias_sample_kernel(tabt_ref, bucket_ref, o_ref):
    b = bucket_ref[...]
    acc = jnp.zeros(b.shape, F32)
    for kk in range(N_BUCKETS):
        acc = jnp.where(b == kk, tabt_ref[:, kk:kk + 1], acc)
    o_ref[...] = jnp.where(b == MASKED_BUCKET, NEG, acc)


def _bias_sample(table, past_len, dec_seq):
    rows = dec_seq * N_HEADS
    qi = (np.arange(rows) // N_HEADS)[:, None]
    qh = (np.arange(rows) % N_HEADS)[:, None]
    col = np.arange(past_len * N_HEADS)[None, :]
    past = np.where(col % N_HEADS == qh, _bucket_np(past_len + qi - col // N_HEADS), MASKED_BUCKET)
    col = np.arange(LANES)[None, :]
    own_ok = (col % N_HEADS == qh) & (col // N_HEADS <= qi)
    own = np.where(own_ok, _bucket_np(qi - col // N_HEADS), MASKED_BUCKET)
    buckets = jnp.asarray(np.concatenate([past, own], axis=1).astype(np.int32))
    tab_t = jnp.tile(table, (1, dec_seq)).T
    return pl.pallas_call(
        _bias_sample_kernel,
        out_shape=jax.ShapeDtypeStruct(buckets.shape, F32),
        name="bias_sample",
    )(tab_t, buckets)


def _attn_prompt_kernel(q_ref, k_ref, v_ref, bias_ref, o_ref, ka_sc, vb_sc, km_sc, s_sc):
    blk = MOBA_BLOCK
    seq = k_ref.shape[0]
    nb = seq // blk
    mrows = km_sc.shape[0]
    scale = HEAD_DIM ** -0.5
    nt = (((1,), (1,)), ((), ()))

    k = k_ref[...]
    ka_sc[:, :HEAD_DIM] = k.astype(BF16)
    row_blk = lax.broadcasted_iota(I32, (seq, LANES), 0) // blk
    ka_sc[:, HEAD_DIM:] = (lax.broadcasted_iota(I32, (seq, LANES), 1) == row_blk).astype(BF16)
    vb_sc[...] = v_ref[...].astype(BF16)
    means = [jnp.mean(k[n * blk:(n + 1) * blk, :], axis=0, keepdims=True) for n in range(nb)]
    means.append(jnp.zeros((mrows - nb, HEAD_DIM), F32))
    km_sc[...] = jnp.concatenate(means, axis=0).astype(BF16)
    sub = lax.broadcasted_iota(I32, (mrows, blk), 0).astype(F32)

    def rows(n):
        return slice(n * blk, (n + 1) * blk)

    slot = 0
    for j in range(nb):
        qb = q_ref[rows(j), :].astype(BF16)
        if j > MOBA_TOPK:
            gates = lax.dot_general(km_sc[...], qb, nt, preferred_element_type=F32)
            gates = jnp.where(sub < float(j), gates, NEG)
            allow = jnp.full((mrows, blk), NEG, F32)
            for _ in range(MOBA_TOPK):
                best = jnp.max(gates, axis=0, keepdims=True)
                first = jnp.min(jnp.where(gates == best, sub, float(mrows)), axis=0, keepdims=True)
                hit = sub == first
                allow = jnp.where(hit, 0.0, allow)
                gates = jnp.where(hit, PICKED, gates)
            allow = jnp.concatenate([allow, jnp.full((LANES - mrows, blk), NEG, F32)], axis=0).T
            q_past = jnp.concatenate([qb, allow.astype(BF16)], axis=1)
            k_cols = slice(0, HEAD_DIM + LANES)
        else:
            q_past = qb
            k_cols = slice(0, HEAD_DIM)

        s_own = lax.dot_general(qb, ka_sc[rows(j), :HEAD_DIM], nt, preferred_element_type=F32) * scale
        s_own = s_own + bias_ref[0, 0]
        run = s_own
        base = slot
        for n in range(j):
            s = lax.dot_general(q_past, ka_sc[rows(n), k_cols], nt, preferred_element_type=F32) * scale
            s = s + bias_ref[0, min(j - n, 2)]
            s_sc[base + n] = s
            run = jnp.maximum(run, s)
        slot += j
        m = jnp.max(run, axis=-1, keepdims=True)
        p = jnp.exp(s_own - m)
        psum = p
        acc = jnp.dot(p.astype(BF16), vb_sc[rows(j), :], preferred_element_type=F32)
        for n in range(j):
            p = jnp.exp(s_sc[base + n] - m)
            psum = psum + p
            acc = acc + jnp.dot(p.astype(BF16), vb_sc[rows(n), :], preferred_element_type=F32)
        o_ref[rows(j), :] = acc / jnp.sum(psum, axis=-1, keepdims=True)


def _attn_prompt(q, k, v, bias_tiles, batch, seq):
    nb = seq // MOBA_BLOCK
    spec = pl.BlockSpec((seq, HEAD_DIM), lambda b, h: (b, h))
    return pl.pallas_call(
        _attn_prompt_kernel,
        grid=(batch, N_HEADS),
        in_specs=[spec, spec, spec,
                  pl.BlockSpec((1, 3, MOBA_BLOCK, MOBA_BLOCK), lambda b, h: (h, 0, 0, 0))],
        out_specs=spec,
        out_shape=jax.ShapeDtypeStruct(q.shape, F32),
        scratch_shapes=[pltpu.VMEM((seq, HEAD_DIM + LANES), BF16), pltpu.VMEM((seq, HEAD_DIM), BF16),
                        pltpu.VMEM((pl.cdiv(nb, BF16_SUBLANES) * BF16_SUBLANES, HEAD_DIM), BF16),
                        pltpu.VMEM((max(nb * (nb - 1) // 2, 1), MOBA_BLOCK, MOBA_BLOCK), F32)],
        compiler_params=_cparams(("parallel", "parallel"), 40),
        name="attn_prompt",
    )(q, k, v, bias_tiles)


PAGES_PER_STEP = 16


def _attn_sample_kernel(pt_ref, q_ref, kn_ref, vn_ref, bias_ref, *refs, n_pages):
    del pt_ref
    pps = PAGES_PER_STEP
    k_refs = refs[:pps]
    v_refs = refs[pps:2 * pps]
    o_ref, s_sc, acc_sc, den_sc = refs[2 * pps:]
    s = pl.program_id(1)
    k_steps = n_pages // pps
    rows = q_ref.shape[0]
    cols = PAGE_SIZE * N_HEADS
    n_blocks = n_pages * PAGE_SIZE // MOBA_BLOCK
    ppb = MOBA_BLOCK // PAGE_SIZE
    scale = HEAD_DIM ** -0.5
    nt = (((1,), (1,)), ((), ()))

    @pl.when(s < k_steps)
    def _():
        qb = q_ref[...].astype(BF16)
        for pg in range(pps):
            kp = k_refs[pg][...].astype(BF16)
            s_sc[s * pps + pg] = lax.dot_general(qb, kp, nt, preferred_element_type=F32)

    @pl.when(s == k_steps)
    def _():
        same_head = (lax.broadcasted_iota(I32, (rows, cols), 1) % N_HEADS
                     == lax.broadcasted_iota(I32, (rows, cols), 0) % N_HEADS)
        gates = []
        for n in range(n_blocks):
            tot = s_sc[n * ppb]
            for t in range(1, ppb):
                tot = tot + s_sc[n * ppb + t]
            gates.append(jnp.sum(jnp.where(same_head, tot, 0.0), axis=-1, keepdims=True))
        allow = []
        for n in range(n_blocks):
            ahead = jnp.zeros((rows, 1), F32)
            for mm in range(n_blocks):
                if mm == n:
                    continue
                before = (gates[mm] >= gates[n]) if mm < n else (gates[mm] > gates[n])
                ahead = ahead + before.astype(F32)
            allow.append(jnp.where(ahead < MOBA_TOPK, 0.0, NEG))
        qb = q_ref[...].astype(BF16)
        past_cols = n_pages * cols
        own = lax.dot_general(qb, kn_ref[...].astype(BF16), nt, preferred_element_type=F32) * scale
        own = own + bias_ref[:, past_cols:]
        run = jnp.full((rows, cols), NEG, F32)
        for t in range(n_pages):
            lg = s_sc[t] * scale + bias_ref[:, t * cols:(t + 1) * cols] + allow[t // ppb]
            s_sc[t] = lg
            run = jnp.maximum(run, lg)
        m = jnp.maximum(jnp.max(own, axis=-1, keepdims=True), jnp.max(run, axis=-1, keepdims=True))
        p_own = jnp.exp(own - m)
        psum = jnp.zeros((rows, cols), F32)
        for t in range(n_pages):
            p = jnp.exp(s_sc[t] - m)
            s_sc[t] = p
            psum = psum + p
        den = jnp.sum(p_own, axis=-1, keepdims=True) + jnp.sum(psum, axis=-1, keepdims=True)
        den_sc[...] = jnp.broadcast_to(den, den_sc.shape)
        acc_sc[...] = jnp.dot(p_own.astype(BF16), vn_ref[...].astype(BF16), preferred_element_type=F32)

    @pl.when(s >= k_steps)
    def _():
        acc = acc_sc[...]
        for pg in range(pps):
            p = s_sc[(s - k_steps) * pps + pg].astype(BF16)
            acc = acc + jnp.dot(p, v_refs[pg][...].astype(BF16), preferred_element_type=F32)
        acc_sc[...] = acc

    @pl.when(s == 2 * k_steps - 1)
    def _():
        o_ref[...] = acc_sc[...] / den_sc[...]


def _attn_sample(q, k_new, v_new, cache_k, cache_v, page_table, bias_s):
    b, rows, hd = q.shape
    n_pages = page_table.shape[1]
    pps = PAGES_PER_STEP
    assert n_pages % pps == 0
    k_steps = n_pages // pps
    cols = PAGE_SIZE * N_HEADS
    q_spec = pl.BlockSpec((None, rows, hd), lambda i, s, pt: (i, 0, 0))
    new_spec = pl.BlockSpec((None, LANES, hd), lambda i, s, pt: (i, 0, 0))

    def k_spec(pg):
        return pl.BlockSpec((None, cols, hd),
                            lambda i, s, pt: (pt[i * n_pages + jnp.minimum(s, k_steps - 1) * pps + pg], 0, 0))

    def v_spec(pg):
        def index(i, s, pt):
            seq = jnp.where(s >= k_steps, i, jnp.maximum(i - 1, 0))
            group = jnp.where(s >= k_steps, s - k_steps, k_steps - 1)
            return (pt[seq * n_pages + group * pps + pg], 0, 0)
        return pl.BlockSpec((None, cols, hd), index)

    grid_spec = pltpu.PrefetchScalarGridSpec(
        num_scalar_prefetch=1,
        grid=(b, 2 * k_steps),
        in_specs=[q_spec, new_spec, new_spec,
                  pl.BlockSpec(bias_s.shape, lambda i, s, pt: (0, 0))]
                 + [k_spec(pg) for pg in range(pps)] + [v_spec(pg) for pg in range(pps)],
        out_specs=q_spec,
        scratch_shapes=[pltpu.VMEM((n_pages, rows, cols), F32),
                        pltpu.VMEM((rows, hd), F32),
                        pltpu.VMEM((rows, hd), F32)],
    )
    return pl.pallas_call(
        functools.partial(_attn_sample_kernel, n_pages=n_pages),
        grid_spec=grid_spec,
        out_shape=jax.ShapeDtypeStruct(q.shape, F32),
        compiler_params=_cparams(("parallel", "arbitrary"), 48),
        name="attn_sample",
    )(page_table.reshape(-1), q, k_new, v_new, bias_s, *([cache_k] * pps), *([cache_v] * pps))


def _mixout_kernel(attn_ref, u_ref, vs_ref, x_ref, gt_ref, sc_ref, sh_ref, wmix_ref, mask_ref, bmix_ref,
                   ga_ref, gs_ref, wout_ref, gffn_ref, wr_ref, br_ref, *rest):
    x1_ref, h2_ref, topi_ref, topg_ref, sgu_sc = rest[-5:]
    tm = x_ref.shape[0]
    for g in range(N_GROUPS):
        cols = slice(g * LANES, (g + 1) * LANES)
        wg = (wmix_ref[g] * mask_ref[...]).astype(BF16)
        for c in range(tm // SGU_CHUNK):
            rws = slice(c * SGU_CHUNK, (c + 1) * SGU_CHUNK)
            mixed = jnp.dot(wg, vs_ref[rws, cols].astype(BF16), preferred_element_type=F32) + bmix_ref[:, cols]
            sgu_sc[rws, cols] = u_ref[rws, cols] * mixed
    an = _rms(attn_ref[...], ga_ref[...]).astype(BF16)
    sn = _rms(sgu_sc[...], gs_ref[...]).astype(BF16)
    out = (jnp.dot(an, wout_ref[:ATTN_WIDTH, :], preferred_element_type=F32)
           + jnp.dot(sn, wout_ref[ATTN_WIDTH:, :], preferred_element_type=F32))
    x1 = x_ref[...] + gt_ref[...] * out
    x1_ref[...] = x1
    h2 = _rms(x1, gffn_ref[...]) * (1.0 + sc_ref[...]) + sh_ref[...]
    h2_ref[...] = h2
    logits = jnp.dot(h2.astype(BF16), wr_ref[...], preferred_element_type=F32) + br_ref[...]
    lane = lax.broadcasted_iota(I32, logits.shape, 1).astype(F32)
    idx_out = jnp.zeros(logits.shape, F32)
    val_out = jnp.zeros(logits.shape, F32)
    top = None
    den = None
    for r in range(TOP_K):
        best = jnp.max(logits, axis=-1, keepdims=True)
        first = jnp.min(jnp.where(logits == best, lane, float(LANES)), axis=-1, keepdims=True)
        if r == 0:
            top = best
        e = jnp.exp(best - top)
        den = e if r == 0 else den + e
        idx_out = jnp.where(lane == float(r), first, idx_out)
        val_out = jnp.where(lane == float(r), e, val_out)
        logits = jnp.where(lane == first, PICKED, logits)
    topi_ref[...] = idx_out.astype(I32)
    topg_ref[...] = val_out / den


def _mixout(attn, u, vs, x, mod3, wmix, mixmask, bmix, g_out_attn, g_out_sgu, w_out_bf, g_ffn, wr_pad, br_pad, tm,
            h2_rows, h2_row0, h2_shared=None):
    t, d = x.shape
    g_cnt, r, _ = mod3.shape
    tiles_per_group = (t // tm) // g_cnt
    w = ATTN_WIDTH
    assert h2_row0 % tm == 0
    mod_spec = lambda chunk: pl.BlockSpec((None, r, d), lambda i: (i // tiles_per_group, 0, chunk))
    half = pl.BlockSpec((tm, w), lambda i: (i, 0))
    full = pl.BlockSpec((tm, d), lambda i: (i, 0))
    small = pl.BlockSpec((tm, LANES), lambda i: (i, 0))
    const = lambda shape: pl.BlockSpec(shape, lambda i: (0,) * len(shape))
    in_specs = [half, half, half, full, mod_spec(2), mod_spec(4), mod_spec(3),
                const(wmix.shape), const(mixmask.shape), const(bmix.shape),
                const((1, w)), const((1, w)), const((d, d)), const((1, d)),
                const((d, LANES)), const((1, LANES))]
    args = [attn, u, vs, x, mod3, mod3, mod3, wmix, mixmask, bmix, g_out_attn.reshape(1, w), g_out_sgu.reshape(1, w),
            w_out_bf, g_ffn.reshape(1, d), wr_pad, br_pad]
    aliases = {}
    if h2_shared is not None:
        aliases = {len(args): 1}
        in_specs.append(pl.BlockSpec(memory_space=pl.ANY))
        args.append(h2_shared)
    return pl.pallas_call(
        _mixout_kernel,
        grid=(t // tm,),
        in_specs=in_specs,
        out_specs=[full, pl.BlockSpec((tm, d), lambda i: (i + h2_row0 // tm, 0)), small, small],
        out_shape=[jax.ShapeDtypeStruct((t, d), F32), jax.ShapeDtypeStruct((h2_rows, d), F32),
                   jax.ShapeDtypeStruct((t, LANES), I32), jax.ShapeDtypeStruct((t, LANES), F32)],
        scratch_shapes=[pltpu.VMEM((tm, w), F32)],
        input_output_aliases=aliases,
        compiler_params=_cparams(("parallel",), 48),
        name="mixout",
    )(*args)


def _moe_kernel(src_ref, exp_ref, nv_ref, off_ref, oblk_ref, h_hbm, wg_ref, wu_ref, bg_ref, bu_ref, wd_ref, bd_ref,
                y_ref, x_sc, act_sc, stage_sc, sem):
    del exp_ref, oblk_ref
    sb = pl.program_id(0)
    t = pl.program_id(1)
    n_f = act_sc.shape[0]
    nv = nv_ref[sb]
    n_chunks = (nv + MOE_CHUNK - 1) // MOE_CHUNK

    @pl.when((t == 0) & (nv > 0))
    def _():
        off = off_ref[sb]
        last = off + nv - 1
        n_batches = n_chunks * (MOE_CHUNK // GATHER_ROWS)

        def row_copy(bt, slot, i, u):
            tok = src_ref[jnp.minimum(off + bt * GATHER_ROWS + i * F32_SUBLANES + u, last)]
            return pltpu.make_async_copy(h_hbm.at[pl.ds(tok, 1)], stage_sc.at[slot, i, pl.ds(u, 1)], sem.at[slot])

        def issue(bt, slot):
            def group(i, c):
                for u in range(F32_SUBLANES):
                    row_copy(bt, slot, i, u).start()
                return c
            lax.fori_loop(0, GATHER_ROWS // F32_SUBLANES, group, 0)

        def drain(bt, slot):
            def group(i, c):
                for u in range(F32_SUBLANES):
                    row_copy(bt, slot, i, u).wait()
                return c
            lax.fori_loop(0, GATHER_ROWS // F32_SUBLANES, group, 0)
            rows = stage_sc[slot].reshape(GATHER_ROWS, x_sc.shape[1])
            x_sc[pl.ds(pl.multiple_of(bt * GATHER_ROWS, GATHER_ROWS), GATHER_ROWS), :] = rows.astype(BF16)

        n_slots = stage_sc.shape[0]
        ahead = n_slots - 1
        for bt in range(ahead):
            @pl.when(bt < n_batches)
            def _():
                issue(bt, bt)

        def body(bt, c):
            @pl.when(bt + ahead < n_batches)
            def _():
                issue(bt + ahead, (bt + ahead) % n_slots)

            drain(bt, bt % n_slots)
            return c

        lax.fori_loop(0, n_batches, body, 0)

    def cascade(fn):
        big = 8 * MOE_CHUNK
        n_big = n_chunks // 8

        def body(i, c):
            fn(pl.multiple_of(i * big, big), big)
            return c

        lax.fori_loop(0, n_big, body, 0)
        rem = n_chunks - n_big * 8
        base = n_big * big
        for part in (4, 2, 1):
            done = (rem // (2 * part)) * (2 * part)

            @pl.when((rem // part) % 2 == 1)
            def _(part=part, done=done):
                fn(pl.multiple_of(base + done * MOE_CHUNK, MOE_CHUNK), part * MOE_CHUNK)

    @pl.when(t < n_f)
    def _():
        bg = bg_ref[...]
        bu = bu_ref[...]

        def up_proj(start, size):
            xb = x_sc[pl.ds(start, size), :]
            g = jnp.dot(xb, wg_ref[...].astype(BF16), preferred_element_type=F32) + bg
            up = jnp.dot(xb, wu_ref[...].astype(BF16), preferred_element_type=F32) + bu
            g = jnp.minimum(g, SWIGLU_LIMIT)
            up = jnp.clip(up, -SWIGLU_LIMIT, SWIGLU_LIMIT)
            act = g * jax.nn.sigmoid(SWIGLU_ALPHA * g) * (up + 1.0)
            act_sc[t, pl.ds(start, size), :] = act.astype(BF16)

        cascade(up_proj)

    @pl.when(t >= n_f)
    def _():
        bd = bd_ref[...]

        def down_proj(start, size):
            acc = jnp.zeros((size, MOE_TILE), F32) + bd
            for f in range(n_f):
                wd = wd_ref[f * MOE_TILE:(f + 1) * MOE_TILE, :].astype(BF16)
                acc = acc + jnp.dot(act_sc[f, pl.ds(start, size), :], wd, preferred_element_type=F32)
            y_ref[pl.ds(start, size), :] = acc

        cascade(down_proj)


def _moe(h2, src_sorted, sb_expert, sb_rows, sb_off, sb_out, w_gate_up, b_gate_up, w_down, b_down):
    _, d = h2.shape
    n_sb = sb_expert.shape[0]
    e, f_dim = w_down.shape[0], w_down.shape[1]
    n_f = f_dim // MOE_TILE
    n_o = d // MOE_TILE

    def f_tile(s, t, nv):
        return jnp.where(nv[s] > 0, jnp.minimum(t, n_f - 1), n_f - 1)

    def o_tile(s, t, nv):
        return jnp.where(nv[s] > 0, jnp.maximum(t - n_f, 0), n_o - 1)

    grid_spec = pltpu.PrefetchScalarGridSpec(
        num_scalar_prefetch=5,
        grid=(n_sb, n_f + n_o),
        in_specs=[
            pl.BlockSpec(memory_space=pl.ANY),
            pl.BlockSpec((None, d, MOE_TILE), lambda s, t, sr, ex, nv, of, ob: (ex[s], 0, f_tile(s, t, nv))),
            pl.BlockSpec((None, d, MOE_TILE), lambda s, t, sr, ex, nv, of, ob: (ex[s], 0, n_f + f_tile(s, t, nv))),
            pl.BlockSpec((None, 1, MOE_TILE), lambda s, t, sr, ex, nv, of, ob: (ex[s], 0, f_tile(s, t, nv))),
            pl.BlockSpec((None, 1, MOE_TILE), lambda s, t, sr, ex, nv, of, ob: (ex[s], 0, n_f + f_tile(s, t, nv))),
            pl.BlockSpec((None, f_dim, MOE_TILE), lambda s, t, sr, ex, nv, of, ob: (ex[s], 0, o_tile(s, t, nv))),
            pl.BlockSpec((None, 1, MOE_TILE), lambda s, t, sr, ex, nv, of, ob: (ex[s], 0, o_tile(s, t, nv))),
        ],
        out_specs=pl.BlockSpec(
            (MOE_ROWS, MOE_TILE),
            lambda s, t, sr, ex, nv, of, ob: (ob[s], jnp.where(nv[s] > 0, jnp.maximum(t - n_f, 0), 0))),
        scratch_shapes=[pltpu.VMEM((MOE_ROWS, d), BF16),
                        pltpu.VMEM((n_f, MOE_ROWS, MOE_TILE), BF16),
                        pltpu.VMEM((GATHER_SLOTS, GATHER_ROWS // F32_SUBLANES, F32_SUBLANES, d), F32),
                        pltpu.SemaphoreType.DMA((GATHER_SLOTS,))],
    )
    return pl.pallas_call(
        _moe_kernel,
        grid_spec=grid_spec,
        out_shape=jax.ShapeDtypeStruct(((n_sb + 1) * MOE_ROWS, d), F32),
        compiler_params=_cparams(("arbitrary", "arbitrary"), 58),
        name="moe",
    )(src_sorted, sb_expert, sb_rows, sb_off, sb_out, h2, w_gate_up, w_gate_up,
      b_gate_up.reshape(e, 1, -1), b_gate_up.reshape(e, 1, -1), w_down, b_down.reshape(e, 1, -1))


def _combine_kernel(dest_ref, x1_ref, gt_ref, gate_ref, gfin_ref, y_hbm, o_ref, buf, sem, *, tok_offset):
    i = pl.program_id(0)
    n_tiles = pl.num_programs(0)
    tm = x1_ref.shape[0]

    groups = tm // F32_SUBLANES

    def row_copy(tile, slot, g, u, kk):
        row = dest_ref[(tok_offset + tile * tm + g * F32_SUBLANES + u) * TOP_K + kk]
        return pltpu.make_async_copy(y_hbm.at[pl.ds(row, 1)], buf.at[slot, kk, g, pl.ds(u, 1)], sem.at[slot])

    def issue(tile, slot):
        def group(g, c):
            for u in range(F32_SUBLANES):
                for kk in range(TOP_K):
                    row_copy(tile, slot, g, u, kk).start()
            return c
        lax.fori_loop(0, groups, group, 0)

    @pl.when(i == 0)
    def _():
        issue(0, 0)

    @pl.when(i + 1 < n_tiles)
    def _():
        issue(i + 1, (i + 1) % 2)

    slot = i % 2

    def wait_group(g, c):
        for u in range(F32_SUBLANES):
            for kk in range(TOP_K):
                row_copy(i, slot, g, u, kk).wait()
        return c

    lax.fori_loop(0, groups, wait_group, 0)
    gate = gate_ref[...]
    y = jnp.zeros(x1_ref.shape, F32)
    for kk in range(TOP_K):
        y = y + gate[:, kk:kk + 1] * buf[slot, kk].reshape(x1_ref.shape)
    x2 = x1_ref[...] + gt_ref[...] * y
    o_ref[...] = _rms(x2, gfin_ref[...])


def _combine(x1, mod3, y_rows, dest, gates, g_final, tm, tok_offset):
    t, d = x1.shape
    g_cnt, r, _ = mod3.shape
    tiles_per_group = (t // tm) // g_cnt
    full = pl.BlockSpec((tm, d), lambda i, ds: (i, 0))
    grid_spec = pltpu.PrefetchScalarGridSpec(
        num_scalar_prefetch=1,
        grid=(t // tm,),
        in_specs=[full,
                  pl.BlockSpec((None, r, d), lambda i, ds: (i // tiles_per_group, 0, 5)),
                  pl.BlockSpec((tm, LANES), lambda i, ds: (i, 0)),
                  pl.BlockSpec((1, d), lambda i, ds: (0, 0)),
                  pl.BlockSpec(memory_space=pl.ANY)],
        out_specs=full,
        scratch_shapes=[pltpu.VMEM((2, TOP_K, tm // F32_SUBLANES, F32_SUBLANES, d), F32),
                        pltpu.SemaphoreType.DMA((2,))],
    )
    return pl.pallas_call(
        functools.partial(_combine_kernel, tok_offset=tok_offset),
        grid_spec=grid_spec,
        out_shape=jax.ShapeDtypeStruct((t, d), F32),
        compiler_params=_cparams(("arbitrary",), 40),
        name="combine",
    )(dest, x1, mod3, gates, g_final.reshape(1, d), y_rows)


def _route(top_idx):
    n = top_idx.shape[0]
    n_assign = n * TOP_K
    n_sb = n_assign // MOE_ROWS + N_EXPERTS
    e_flat = top_idx.reshape(-1)
    onehot = (e_flat[:, None] == jnp.arange(N_EXPERTS)[None, :]).astype(I32)
    csum = jnp.cumsum(onehot, axis=0)
    counts = csum[-1]
    rank = jnp.take_along_axis(csum, e_flat[:, None], axis=1)[:, 0] - 1
    seg_start = jnp.cumsum(counts) - counts
    bits = max(n_assign - 1, 1).bit_length()
    assert N_EXPERTS << bits < 2 ** 31
    keys = jnp.sort(e_flat.astype(I32) * (1 << bits) + jnp.arange(n_assign, dtype=I32))
    src_sorted = ((keys & ((1 << bits) - 1)) // TOP_K).astype(I32)
    sb_per_e = (counts + MOE_ROWS - 1) // MOE_ROWS
    sb_end = jnp.cumsum(sb_per_e)
    sb_start = sb_end - sb_per_e
    n_used = sb_end[-1]
    dest = (sb_start[e_flat] * MOE_ROWS + rank).astype(I32)
    sb = jnp.arange(n_sb)
    sb_clamped = jnp.minimum(sb, n_used - 1)
    sb_expert = jnp.minimum(jnp.searchsorted(sb_end, sb_clamped, side='right'), N_EXPERTS - 1).astype(I32)
    local = sb_clamped - sb_start[sb_expert]
    sb_rows = jnp.where(sb < n_used, jnp.clip(counts[sb_expert] - local * MOE_ROWS, 0, MOE_ROWS), 0).astype(I32)
    sb_off = (seg_start[sb_expert] + local * MOE_ROWS).astype(I32)
    sb_out = jnp.concatenate([jnp.where(sb < n_used, sb, n_sb), (n_used - 1)[None]]).astype(I32)
    return dest, src_sorted, sb_expert, sb_rows, sb_off, sb_out


def kernel(x_prompt, x_sample, cache_k, cache_v, page_table, c_prompt, c_sample, rel_bias_table, w_ada, b_ada,
           g_mix, w_in, g_sgu, w_sp, b_sp, g_out_attn, g_out_sgu, w_out, g_ffn, w_router, b_router,
           w_gate_up, b_gate_up, w_down, b_down, g_final):
    batch, seq, d = x_prompt.shape
    dec_batch, dec_seq, _ = x_sample.shape
    depth = w_ada.shape[0]
    assert depth == 1 and d == D_MODEL and seq % MOBA_BLOCK == 0 and seq // MOBA_BLOCK <= LANES
    n_p = batch * seq
    n_s = dec_batch * dec_seq
    past_len = page_table.shape[1] * PAGE_SIZE
    assert past_len % MOBA_BLOCK == 0 and dec_seq <= SGU_CHUNK and SGU_CHUNK % dec_seq == 0
    width = N_HEADS * HEAD_DIM
    drop = lambda a: a.reshape(a.shape[1:])

    mod = _adaln(jnp.concatenate([c_prompt, c_sample], axis=0), drop(w_ada), drop(b_ada))
    mod_p = mod[:batch].reshape(batch, 1, 6 * d)
    tm_s = min(n_s, 256)
    assert n_s % tm_s == 0 and tm_s % SGU_CHUNK == 0
    mod_s = jnp.repeat(mod[batch:], dec_seq, axis=0).reshape(n_s // tm_s, tm_s, 6 * d)

    w_in_bf = drop(w_in).astype(BF16)
    w_out_bf = drop(w_out).astype(BF16)
    xp = x_prompt.reshape(n_p, d)
    xs = x_sample.reshape(n_s, d)
    g_sgu_flat = g_sgu.reshape(-1)
    g_mix1, g_ffn1, g_oa, g_os = drop(g_mix), drop(g_ffn), drop(g_out_attn), drop(g_out_sgu)
    w_sp1, b_sp1 = drop(w_sp), drop(b_sp)

    q_p, k_p, v_p, u_p, vs_p = _inproj(xp, mod_p, g_mix1, w_in_bf, g_sgu_flat, 512)
    q_s, k_s, v_s, u_s, vs_s = _inproj(xs, mod_s, g_mix1, w_in_bf, g_sgu_flat, tm_s)

    attn_p = _attn_prompt(q_p, k_p, v_p, _bias_prompt(rel_bias_table), batch, seq)
    qh_rows = dec_seq * N_HEADS
    assert qh_rows <= LANES
    per_head = lambda a: a.reshape(dec_batch, qh_rows, HEAD_DIM)
    pad_new = lambda a: jnp.pad(per_head(a), ((0, 0), (0, LANES - qh_rows), (0, 0)))
    page_rows = PAGE_SIZE * N_HEADS
    attn_s = _attn_sample(per_head(q_s), pad_new(k_s), pad_new(v_s),
                          cache_k.reshape(-1, page_rows, HEAD_DIM), cache_v.reshape(-1, page_rows, HEAD_DIM),
                          page_table, _bias_sample(rel_bias_table, past_len, dec_seq)).reshape(n_s, width)

    tri = jnp.tril(jnp.ones((SGU_CHUNK, SGU_CHUNK), F32))
    reps = SGU_CHUNK // dec_seq
    seq_id = jnp.arange(SGU_CHUNK) // dec_seq
    mask_s = tri * (seq_id[:, None] == seq_id[None, :]).astype(F32)
    wmix_s = jnp.tile(w_sp1[:, :dec_seq, :dec_seq], (1, reps, reps))
    bmix_p = jnp.repeat(b_sp1.T, SGU_WIDTH // N_GROUPS, axis=1)
    bmix_s = jnp.repeat(jnp.tile(b_sp1[:, :dec_seq], (1, reps)).T, SGU_WIDTH // N_GROUPS, axis=1)
    wr_pad = jnp.pad(drop(w_router), ((0, 0), (0, LANES - N_EXPERTS))).astype(BF16)
    br_pad = jnp.pad(drop(b_router), (0, LANES - N_EXPERTS), constant_values=NEG).reshape(1, LANES)

    assert n_p % tm_s == 0
    x1_p, h2_part, ti_p, tg_p = _mixout(attn_p, u_p, vs_p, xp, mod_p, w_sp1, tri, bmix_p, g_oa,
                                        g_os, w_out_bf, g_ffn1, wr_pad, br_pad, 256, n_p + n_s, 0)
    x1_s, h2, ti_s, tg_s = _mixout(attn_s, u_s, vs_s, xs, mod_s, wmix_s, mask_s, bmix_s, g_oa,
                                   g_os, w_out_bf, g_ffn1, wr_pad, br_pad, tm_s, n_p + n_s, n_p, h2_part)
    top_idx = jnp.concatenate([ti_p, ti_s], axis=0)[:, :TOP_K]
    dest, src_sorted, sb_expert, sb_rows, sb_off, sb_out = _route(top_idx)
    y_rows = _moe(h2, src_sorted, sb_expert, sb_rows, sb_off, sb_out,
                  drop(w_gate_up), drop(b_gate_up), drop(w_down), drop(b_down))

    y_p = _combine(x1_p, mod_p, y_rows, dest, tg_p, g_final, 256, 0)
    y_s = _combine(x1_s, mod_s, y_rows, dest, tg_s, g_final, tm_s, n_p)

    hd = (N_HEADS, HEAD_DIM)
    gd = (N_GROUPS, SGU_WIDTH // N_GROUPS)
    last_chunk = ((seq - 1) // SGU_CHUNK) * SGU_CHUNK
    return (y_p.reshape(batch, seq, d), y_s.reshape(dec_batch, dec_seq, d),
            k_p.reshape(1, batch, seq, *hd), v_p.reshape(1, batch, seq, *hd),
            k_s.reshape(1, dec_batch, dec_seq, *hd), v_s.reshape(1, dec_batch, dec_seq, *hd),
            vs_p.reshape(batch, seq, *gd)[:, last_chunk:][None], vs_s.reshape(1, dec_batch, dec_seq, *gd))
```

```python
import functools
import math

import numpy as np
import jax
import jax.numpy as jnp
from jax import lax
from jax.experimental import pallas as pl
from jax.experimental.pallas import tpu as pltpu

F32 = jnp.float32
BF16 = jnp.bfloat16
I32 = jnp.int32

D_MODEL = 2048
ATTN_WIDTH = 1024
SGU_WIDTH = 1024
HEAD_DIM = 128
N_HEADS = 8
N_GROUPS = 8
MOBA_BLOCK = 256
MOBA_TOPK = 3
SGU_CHUNK = 128
N_BUCKETS = 32
MAX_DISTANCE = 128
N_EXPERTS = 32
TOP_K = 4
SWIGLU_ALPHA = 1.702
SWIGLU_LIMIT = 7.0
RMS_EPS = 1e-5
NEG = -1e30
PICKED = -3e38
PAGE_SIZE = 128
LANES = 128
MASKED_BUCKET = N_BUCKETS

MOE_ROWS = 3072
MOE_CHUNK = 256
MOE_TILE = 256
GATHER_ROWS = 128
GATHER_SLOTS = 6
F32_SUBLANES = 8
BF16_SUBLANES = 16


def _cparams(sem, vmem_mb):
    return pltpu.CompilerParams(dimension_semantics=sem, vmem_limit_bytes=vmem_mb << 20)


def _rms(x, g):
    return x * lax.rsqrt(jnp.mean(x * x, axis=-1, keepdims=True) + RMS_EPS) * g


def _gelu(x):
    return 0.5 * x * (1.0 + lax.erf(x * (2.0 ** -0.5)))


def _adaln_kernel(c_ref, w_ref, b_ref, o_ref):
    c = c_ref[...]
    a = (c * jax.nn.sigmoid(c)).astype(BF16)
    o_ref[...] = jnp.dot(a, w_ref[...].astype(BF16), preferred_element_type=F32) + b_ref[...]


def _adaln(c, w_ada, b_ada):
    r, d = c.shape
    n = w_ada.shape[1]
    tn = 1024
    return pl.pallas_call(
        _adaln_kernel,
        grid=(n // tn,),
        in_specs=[pl.BlockSpec((r, d), lambda i: (0, 0)),
                  pl.BlockSpec((d, tn), lambda i: (0, i)),
                  pl.BlockSpec((1, tn), lambda i: (0, i))],
        out_specs=pl.BlockSpec((r, tn), lambda i: (0, i)),
        out_shape=jax.ShapeDtypeStruct((r, n), F32),
        compiler_params=_cparams(("arbitrary",), 40),
        name="adaln",
    )(c, w_ada, b_ada.reshape(1, n))


def _inproj_kernel(x_ref, sh_ref, sc_ref, gmix_ref, w_ref, gsgu_ref,
                   q_ref, k_ref, v_ref, u_ref, vs_ref, h_sc):
    n = pl.program_id(1)

    @pl.when(n == 0)
    def _():
        h = _rms(x_ref[...], gmix_ref[...]) * (1.0 + sc_ref[...]) + sh_ref[...]
        h_sc[...] = h.astype(BF16)

    def proj(cols):
        return jnp.dot(h_sc[...], w_ref[:, cols], preferred_element_type=F32)

    for idx, out_ref in enumerate((q_ref, k_ref, v_ref)):
        @pl.when(n == idx)
        def _(out_ref=out_ref):
            out_ref[...] = proj(slice(None))

    slab = 2 * LANES

    @pl.when(n == 3)
    def _():
        for c in range(0, ATTN_WIDTH, slab):
            u_ref[:, c:c + slab] = _gelu(proj(slice(c, c + slab)))

    @pl.when(n == 4)
    def _():
        for c in range(0, ATTN_WIDTH, slab):
            g = _gelu(proj(slice(c, c + slab)))
            for i in range(slab // LANES):
                sl = slice(c + i * LANES, c + (i + 1) * LANES)
                vs_ref[:, sl] = _rms(g[:, i * LANES:(i + 1) * LANES], gsgu_ref[:, sl])


def _inproj(x, mod3, g_mix, w_in_bf, g_sgu, tm):
    t, d = x.shape
    g_cnt, r, _ = mod3.shape
    tiles_per_group = (t // tm) // g_cnt
    w = ATTN_WIDTH
    mod_spec = lambda chunk: pl.BlockSpec((None, r, d), lambda i, n: (i // tiles_per_group, 0, chunk))
    out_spec = pl.BlockSpec((tm, w), lambda i, n: (i, 0))
    out_sds = jax.ShapeDtypeStruct((t, w), F32)
    return pl.pallas_call(
        _inproj_kernel,
        grid=(t // tm, 5),
        in_specs=[pl.BlockSpec((tm, d), lambda i, n: (i, 0)),
                  mod_spec(0), mod_spec(1),
                  pl.BlockSpec((1, d), lambda i, n: (0, 0)),
                  pl.BlockSpec((d, w), lambda i, n: (0, n)),
                  pl.BlockSpec((1, w), lambda i, n: (0, 0))],
        out_specs=[out_spec] * 5,
        out_shape=[out_sds] * 5,
        scratch_shapes=[pltpu.VMEM((tm, d), BF16)],
        compiler_params=_cparams(("parallel", "arbitrary"), 56),
        name="inproj",
    )(x, mod3, mod3, g_mix.reshape(1, d), w_in_bf, g_sgu.reshape(1, w))


def _bucket_np(dist):
    dist = np.maximum(dist, 0)
    max_exact = N_BUCKETS // 2
    dist_f = np.maximum(dist, max_exact).astype(np.float32)
    large = max_exact + (np.log(dist_f / np.float32(max_exact)) / np.float32(math.log(MAX_DISTANCE / max_exact))
                         * np.float32(N_BUCKETS - max_exact)).astype(np.int32)
    return np.where(dist < max_exact, dist, np.minimum(large, N_BUCKETS - 1)).astype(np.int32)


def _bias_prompt_kernel(tab_ref, bucket_ref, o_ref):
    h = pl.program_id(0)
    b = bucket_ref[...]
    acc = jnp.zeros(b.shape, F32)
    for kk in range(N_BUCKETS):
        acc = jnp.where(b == kk, tab_ref[kk, h], acc)
    o_ref[0] = jnp.where(b == MASKED_BUCKET, NEG, acc)


def _bias_prompt(table):
    r = np.arange(MOBA_BLOCK)[:, None]
    c = np.arange(MOBA_BLOCK)[None, :]
    own = np.where(c <= r, _bucket_np(r - c), MASKED_BUCKET)
    prev = _bucket_np(MOBA_BLOCK + r - c)
    far = _bucket_np(2 * MOBA_BLOCK + r - c)
    assert (far == far[0, 0]).all()
    buckets = jnp.asarray(np.stack([own, prev, far]).astype(np.int32))
    return pl.pallas_call(
        _bias_prompt_kernel,
        grid=(N_HEADS,),
        in_specs=[pl.BlockSpec(memory_space=pltpu.SMEM),
                  pl.BlockSpec((3, MOBA_BLOCK, MOBA_BLOCK), lambda h: (0, 0, 0))],
        out_specs=pl.BlockSpec((1, 3, MOBA_BLOCK, MOBA_BLOCK), lambda h: (h, 0, 0, 0)),
        out_shape=jax.ShapeDtypeStruct((N_HEADS, 3, MOBA_BLOCK, MOBA_BLOCK), F32),
        compiler_params=_cparams(("arbitrary",), 16),
        name="bias_prompt",
    )(table, buckets)


def _bias_sample_kernel(tabt_ref, bucket_ref, o_ref):
    b = bucket_ref[...]
    acc = jnp.zeros(b.shape, F32)
    for kk in range(N_BUCKETS):
        acc = jnp.where(b == kk, tabt_ref[:, kk:kk + 1], acc)
    o_ref[...] = jnp.where(b == MASKED_BUCKET, NEG, acc)


def _bias_sample(table, past_len, dec_seq):
    rows = dec_seq * N_HEADS
    qi = (np.arange(rows) // N_HEADS)[:, None]
    qh = (np.arange(rows) % N_HEADS)[:, None]
    col = np.arange(past_len * N_HEADS)[None, :]
    past = np.where(col % N_HEADS == qh, _bucket_np(past_len + qi - col // N_HEADS), MASKED_BUCKET)
    col = np.arange(LANES)[None, :]
    own_ok = (col % N_HEADS == qh) & (col // N_HEADS <= qi)
    own = np.where(own_ok, _bucket_np(qi - col // N_HEADS), MASKED_BUCKET)
    buckets = jnp.asarray(np.concatenate([past, own], axis=1).astype(np.int32))
    tab_t = jnp.tile(table, (1, dec_seq)).T
    return pl.pallas_call(
        _bias_sample_kernel,
        out_shape=jax.ShapeDtypeStruct(buckets.shape, F32),
        name="bias_sample",
    )(tab_t, buckets)


def _attn_prompt_kernel(q_ref, k_ref, v_ref, bias_ref, o_ref, ka_sc, vb_sc, km_sc, s_sc):
    blk = MOBA_BLOCK
    seq = k_ref.shape[0]
    nb = seq // blk
    mrows = km_sc.shape[0]
    scale = HEAD_DIM ** -0.5
    nt = (((1,), (1,)), ((), ()))

    k = k_ref[...]
    ka_sc[:, :HEAD_DIM] = k.astype(BF16)
    row_blk = lax.broadcasted_iota(I32, (seq, LANES), 0) // blk
    ka_sc[:, HEAD_DIM:] = (lax.broadcasted_iota(I32, (seq, LANES), 1) == row_blk).astype(BF16)
    vb_sc[...] = v_ref[...].astype(BF16)
    means = [jnp.mean(k[n * blk:(n + 1) * blk, :], axis=0, keepdims=True) for n in range(nb)]
    means.append(jnp.zeros((mrows - nb, HEAD_DIM), F32))
    km_sc[...] = jnp.concatenate(means, axis=0).astype(BF16)
    sub = lax.broadcasted_iota(I32, (mrows, blk), 0).astype(F32)

    def rows(n):
        return slice(n * blk, (n + 1) * blk)

    slot = 0
    for j in range(nb):
        qb = q_ref[rows(j), :].astype(BF16)
        if j > MOBA_TOPK:
            gates = lax.dot_general(km_sc[...], qb, nt, preferred_element_type=F32)
            gates = jnp.where(sub < float(j), gates, NEG)
            allow = jnp.full((mrows, blk), NEG, F32)
            for _ in range(MOBA_TOPK):
                best = jnp.max(gates, axis=0, keepdims=True)
                first = jnp.min(jnp.where(gates == best, sub, float(mrows)), axis=0, keepdims=True)
                hit = sub == first
                allow = jnp.where(hit, 0.0, allow)
                gates = jnp.where(hit, PICKED, gates)
            allow = jnp.concatenate([allow, jnp.full((LANES - mrows, blk), NEG, F32)], axis=0).T
            q_past = jnp.concatenate([qb, allow.astype(BF16)], axis=1)
            k_cols = slice(0, HEAD_DIM + LANES)
        else:
            q_past = qb
            k_cols = slice(0, HEAD_DIM)

        s_own = lax.dot_general(qb, ka_sc[rows(j), :HEAD_DIM], nt, preferred_element_type=F32) * scale
        s_own = s_own + bias_ref[0, 0]
        run = s_own
        base = slot
        for n in range(j):
            s = lax.dot_general(q_past, ka_sc[rows(n), k_cols], nt, preferred_element_type=F32) * scale
            s = s + bias_ref[0, min(j - n, 2)]
            s_sc[base + n] = s
            run = jnp.maximum(run, s)
        slot += j
        m = jnp.max(run, axis=-1, keepdims=True)
        p = jnp.exp(s_own - m)
        psum = p
        acc = jnp.dot(p.astype(BF16), vb_sc[rows(j), :], preferred_element_type=F32)
        for n in range(j):
            p = jnp.exp(s_sc[base + n] - m)
            psum = psum + p
            acc = acc + jnp.dot(p.astype(BF16), vb_sc[rows(n), :], preferred_element_type=F32)
        o_ref[rows(j), :] = acc / jnp.sum(psum, axis=-1, keepdims=True)


def _attn_prompt(q, k, v, bias_tiles, batch, seq):
    nb = seq // MOBA_BLOCK
    spec = pl.BlockSpec((seq, HEAD_DIM), lambda b, h: (b, h))
    return pl.pallas_call(
        _attn_prompt_kernel,
        grid=(batch, N_HEADS),
        in_specs=[spec, spec, spec,
                  pl.BlockSpec((1, 3, MOBA_BLOCK, MOBA_BLOCK), lambda b, h: (h, 0, 0, 0))],
        out_specs=spec,
        out_shape=jax.ShapeDtypeStruct(q.shape, F32),
        scratch_shapes=[pltpu.VMEM((seq, HEAD_DIM + LANES), BF16), pltpu.VMEM((seq, HEAD_DIM), BF16),
                        pltpu.VMEM((pl.cdiv(nb, BF16_SUBLANES) * BF16_SUBLANES, HEAD_DIM), BF16),
                        pltpu.VMEM((max(nb * (nb - 1) // 2, 1), MOBA_BLOCK, MOBA_BLOCK), F32)],
        compiler_params=_cparams(("parallel", "parallel"), 40),
        name="attn_prompt",
    )(q, k, v, bias_tiles)


def _attn_sample_kernel(pt_ref, q_ref, kn_ref, vn_ref, bias_ref, ck_hbm, cv_hbm, o_ref,
                        pages_sc, s_sc, sem, *, n_pages):
    i = pl.program_id(0)
    s = pl.program_id(1)
    n_seq = pl.num_programs(0)
    rows = q_ref.shape[0]
    cols = PAGE_SIZE * N_HEADS
    n_blocks = n_pages * PAGE_SIZE // MOBA_BLOCK
    ppb = MOBA_BLOCK // PAGE_SIZE
    scale = HEAD_DIM ** -0.5
    nt = (((1,), (1,)), ((), ()))

    def page_copy(seq, kind, pg):
        cache = cv_hbm if kind else ck_hbm
        return pltpu.make_async_copy(cache.at[pt_ref[seq * n_pages + pg]], pages_sc.at[kind, pg], sem.at[kind])

    def start_pages(seq, kind):
        for pg in range(n_pages):
            page_copy(seq, kind, pg).start()

    def wait_pages(seq, kind):
        for pg in range(n_pages):
            page_copy(seq, kind, pg).wait()

    @pl.when((i == 0) & (s == 0))
    def _():
        start_pages(0, 0)

    @pl.when(s == 0)
    def _():
        start_pages(i, 1)
        wait_pages(i, 0)
        qb = q_ref[...].astype(BF16)
        for pg in range(n_pages):
            s_sc[pg] = lax.dot_general(qb, pages_sc[0, pg].astype(BF16), nt, preferred_element_type=F32)

    @pl.when((s == 1) & (i + 1 < n_seq))
    def _():
        start_pages(i + 1, 0)

    @pl.when(s == 1)
    def _():
        same_head = (lax.broadcasted_iota(I32, (rows, cols), 1) % N_HEADS
                     == lax.broadcasted_iota(I32, (rows, cols), 0) % N_HEADS)
        gates = []
        for n in range(n_blocks):
            tot = s_sc[n * ppb]
            for t in range(1, ppb):
                tot = tot + s_sc[n * ppb + t]
            gates.append(jnp.sum(jnp.where(same_head, tot, 0.0), axis=-1, keepdims=True))
        allow = []
        for n in range(n_blocks):
            ahead = jnp.zeros((rows, 1), F32)
            for mm in range(n_blocks):
                if mm == n:
                    continue
                before = (gates[mm] >= gates[n]) if mm < n else (gates[mm] > gates[n])
                ahead = ahead + before.astype(F32)
            allow.append(jnp.where(ahead < MOBA_TOPK, 0.0, NEG))
        qb = q_ref[...].astype(BF16)
        past_cols = n_pages * cols
        own = lax.dot_general(qb, kn_ref[...].astype(BF16), nt, preferred_element_type=F32) * scale
        own = own + bias_ref[:, past_cols:]
        run = jnp.full((rows, cols), NEG, F32)
        for t in range(n_pages):
            lg = s_sc[t] * scale + bias_ref[:, t * cols:(t + 1) * cols] + allow[t // ppb]
            s_sc[t] = lg
            run = jnp.maximum(run, lg)
        m = jnp.maximum(jnp.max(own, axis=-1, keepdims=True), jnp.max(run, axis=-1, keepdims=True))
        p_own = jnp.exp(own - m)
        psum = jnp.zeros((rows, cols), F32)
        for t in range(n_pages):
            p = jnp.exp(s_sc[t] - m)
            s_sc[t] = p
            psum = psum + p
        den = jnp.sum(p_own, axis=-1, keepdims=True) + jnp.sum(psum, axis=-1, keepdims=True)
        acc = jnp.dot(p_own.astype(BF16), vn_ref[...].astype(BF16), preferred_element_type=F32)
        wait_pages(i, 1)
        for pg in range(n_pages):
            acc = acc + jnp.dot(s_sc[pg].astype(BF16), pages_sc[1, pg].astype(BF16), preferred_element_type=F32)
        o_ref[...] = acc / den


def _attn_sample(q, k_new, v_new, cache_k, cache_v, page_table, bias_s):
    b, rows, hd = q.shape
    n_pages = page_table.shape[1]
    cols = PAGE_SIZE * N_HEADS
    q_spec = pl.BlockSpec((None, rows, hd), lambda i, s, pt: (i, 0, 0))
    new_spec = pl.BlockSpec((None, LANES, hd), lambda i, s, pt: (i, 0, 0))
    grid_spec = pltpu.PrefetchScalarGridSpec(
        num_scalar_prefetch=1,
        grid=(b, 2),
        in_specs=[q_spec, new_spec, new_spec,
                  pl.BlockSpec(bias_s.shape, lambda i, s, pt: (0, 0)),
                  pl.BlockSpec(memory_space=pl.ANY), pl.BlockSpec(memory_space=pl.ANY)],
        out_specs=q_spec,
        scratch_shapes=[pltpu.VMEM((2, n_pages, cols, hd), F32),
                        pltpu.VMEM((n_pages, rows, cols), F32),
                        pltpu.SemaphoreType.DMA((2,))],
    )
    return pl.pallas_call(
        functools.partial(_attn_sample_kernel, n_pages=n_pages),
        grid_spec=grid_spec,
        out_shape=jax.ShapeDtypeStruct(q.shape, F32),
        compiler_params=_cparams(("arbitrary", "arbitrary"), 48),
        name="attn_sample",
    )(page_table.reshape(-1), q, k_new, v_new, bias_s, cache_k, cache_v)


def _mixout_kernel(attn_ref, u_ref, vs_ref, x_ref, gt_ref, sc_ref, sh_ref, wmix_ref, mask_ref, bmix_ref,
                   ga_ref, gs_ref, wout_ref, gffn_ref, wr_ref, br_ref, *rest):
    x1_ref, h2_ref, topi_ref, topg_ref, sgu_sc = rest[-5:]
    tm = x_ref.shape[0]
    for g in range(N_GROUPS):
        cols = slice(g * LANES, (g + 1) * LANES)
        wg = (wmix_ref[g] * mask_ref[...]).astype(BF16)
        for c in range(tm // SGU_CHUNK):
            rws = slice(c * SGU_CHUNK, (c + 1) * SGU_CHUNK)
            mixed = jnp.dot(wg, vs_ref[rws, cols].astype(BF16), preferred_element_type=F32) + bmix_ref[:, cols]
            sgu_sc[rws, cols] = u_ref[rws, cols] * mixed
    an = _rms(attn_ref[...], ga_ref[...]).astype(BF16)
    sn = _rms(sgu_sc[...], gs_ref[...]).astype(BF16)
    out = (jnp.dot(an, wout_ref[:ATTN_WIDTH, :], preferred_element_type=F32)
           + jnp.dot(sn, wout_ref[ATTN_WIDTH:, :], preferred_element_type=F32))
    x1 = x_ref[...] + gt_ref[...] * out
    x1_ref[...] = x1
    h2 = _rms(x1, gffn_ref[...]) * (1.0 + sc_ref[...]) + sh_ref[...]
    h2_ref[...] = h2
    logits = jnp.dot(h2.astype(BF16), wr_ref[...], preferred_element_type=F32) + br_ref[...]
    lane = lax.broadcasted_iota(I32, logits.shape, 1).astype(F32)
    idx_out = jnp.zeros(logits.shape, F32)
    val_out = jnp.zeros(logits.shape, F32)
    top = None
    den = None
    for r in range(TOP_K):
        best = jnp.max(logits, axis=-1, keepdims=True)
        first = jnp.min(jnp.where(logits == best, lane, float(LANES)), axis=-1, keepdims=True)
        if r == 0:
            top = best
        e = jnp.exp(best - top)
        den = e if r == 0 else den + e
        idx_out = jnp.where(lane == float(r), first, idx_out)
        val_out = jnp.where(lane == float(r), e, val_out)
        logits = jnp.where(lane == first, PICKED, logits)
    topi_ref[...] = idx_out.astype(I32)
    topg_ref[...] = val_out / den


def _mixout(attn, u, vs, x, mod3, wmix, mixmask, bmix, g_out_attn, g_out_sgu, w_out_bf, g_ffn, wr_pad, br_pad, tm,
            h2_rows, h2_row0, h2_shared=None):
    t, d = x.shape
    g_cnt, r, _ = mod3.shape
    tiles_per_group = (t // tm) // g_cnt
    w = ATTN_WIDTH
    assert h2_row0 % tm == 0
    mod_spec = lambda chunk: pl.BlockSpec((None, r, d), lambda i: (i // tiles_per_group, 0, chunk))
    half = pl.BlockSpec((tm, w), lambda i: (i, 0))
    full = pl.BlockSpec((tm, d), lambda i: (i, 0))
    small = pl.BlockSpec((tm, LANES), lambda i: (i, 0))
    const = lambda shape: pl.BlockSpec(shape, lambda i: (0,) * len(shape))
    in_specs = [half, half, half, full, mod_spec(2), mod_spec(4), mod_spec(3),
                const(wmix.shape), const(mixmask.shape), const(bmix.shape),
                const((1, w)), const((1, w)), const((d, d)), const((1, d)),
                const((d, LANES)), const((1, LANES))]
    args = [attn, u, vs, x, mod3, mod3, mod3, wmix, mixmask, bmix, g_out_attn.reshape(1, w), g_out_sgu.reshape(1, w),
            w_out_bf, g_ffn.reshape(1, d), wr_pad, br_pad]
    aliases = {}
    if h2_shared is not None:
        aliases = {len(args): 1}
        in_specs.append(pl.BlockSpec(memory_space=pl.ANY))
        args.append(h2_shared)
    return pl.pallas_call(
        _mixout_kernel,
        grid=(t // tm,),
        in_specs=in_specs,
        out_specs=[full, pl.BlockSpec((tm, d), lambda i: (i + h2_row0 // tm, 0)), small, small],
        out_shape=[jax.ShapeDtypeStruct((t, d), F32), jax.ShapeDtypeStruct((h2_rows, d), F32),
                   jax.ShapeDtypeStruct((t, LANES), I32), jax.ShapeDtypeStruct((t, LANES), F32)],
        scratch_shapes=[pltpu.VMEM((tm, w), F32)],
        input_output_aliases=aliases,
        compiler_params=_cparams(("parallel",), 48),
        name="mixout",
    )(*args)


def _moe_kernel(src_ref, exp_ref, nv_ref, off_ref, oblk_ref, h_hbm, wg_ref, wu_ref, bg_ref, bu_ref, wd_ref, bd_ref,
                y_ref, x_sc, act_sc, stage_sc, sem):
    del exp_ref, oblk_ref
    sb = pl.program_id(0)
    t = pl.program_id(1)
    n_f = act_sc.shape[0]
    nv = nv_ref[sb]
    n_chunks = (nv + MOE_CHUNK - 1) // MOE_CHUNK

    @pl.when((t == 0) & (nv > 0))
    def _():
        off = off_ref[sb]
        last = off + nv - 1
        n_batches = n_chunks * (MOE_CHUNK // GATHER_ROWS)

        def row_copy(bt, slot, i, u):
            tok = src_ref[jnp.minimum(off + bt * GATHER_ROWS + i * F32_SUBLANES + u, last)]
            return pltpu.make_async_copy(h_hbm.at[pl.ds(tok, 1)], stage_sc.at[slot, i, pl.ds(u, 1)], sem.at[slot])

        def issue(bt, slot):
            def group(i, c):
                for u in range(F32_SUBLANES):
                    row_copy(bt, slot, i, u).start()
                return c
            lax.fori_loop(0, GATHER_ROWS // F32_SUBLANES, group, 0)

        def drain(bt, slot):
            def group(i, c):
                for u in range(F32_SUBLANES):
                    row_copy(bt, slot, i, u).wait()
                return c
            lax.fori_loop(0, GATHER_ROWS // F32_SUBLANES, group, 0)
            rows = stage_sc[slot].reshape(GATHER_ROWS, x_sc.shape[1])
            x_sc[pl.ds(pl.multiple_of(bt * GATHER_ROWS, GATHER_ROWS), GATHER_ROWS), :] = rows.astype(BF16)

        n_slots = stage_sc.shape[0]
        ahead = n_slots - 1
        for bt in range(ahead):
            @pl.when(bt < n_batches)
            def _():
                issue(bt, bt)

        def body(bt, c):
            @pl.when(bt + ahead < n_batches)
            def _():
                issue(bt + ahead, (bt + ahead) % n_slots)

            drain(bt, bt % n_slots)
            return c

        lax.fori_loop(0, n_batches, body, 0)

    def cascade(fn):
        big = 8 * MOE_CHUNK
        n_big = n_chunks // 8

        def body(i, c):
            fn(pl.multiple_of(i * big, big), big)
            return c

        lax.fori_loop(0, n_big, body, 0)
        rem = n_chunks - n_big * 8
        base = n_big * big
        for part in (4, 2, 1):
            done = (rem // (2 * part)) * (2 * part)

            @pl.when((rem // part) % 2 == 1)
            def _(part=part, done=done):
                fn(pl.multiple_of(base + done * MOE_CHUNK, MOE_CHUNK), part * MOE_CHUNK)

    @pl.when(t < n_f)
    def _():
        bg = bg_ref[...]
        bu = bu_ref[...]

        def up_proj(start, size):
            xb = x_sc[pl.ds(start, size), :]
            g = jnp.dot(xb, wg_ref[...].astype(BF16), preferred_element_type=F32) + bg
            up = jnp.dot(xb, wu_ref[...].astype(BF16), preferred_element_type=F32) + bu
            g = jnp.minimum(g, SWIGLU_LIMIT)
            up = jnp.clip(up, -SWIGLU_LIMIT, SWIGLU_LIMIT)
            act = g * jax.nn.sigmoid(SWIGLU_ALPHA * g) * (up + 1.0)
            act_sc[t, pl.ds(start, size), :] = act.astype(BF16)

        cascade(up_proj)

    @pl.when(t >= n_f)
    def _():
        bd = bd_ref[...]

        def down_proj(start, size):
            acc = jnp.zeros((size, MOE_TILE), F32) + bd
            for f in range(n_f):
                wd = wd_ref[f * MOE_TILE:(f + 1) * MOE_TILE, :].astype(BF16)
                acc = acc + jnp.dot(act_sc[f, pl.ds(start, size), :], wd, preferred_element_type=F32)
            y_ref[pl.ds(start, size), :] = acc

        cascade(down_proj)


def _moe(h2, src_sorted, sb_expert, sb_rows, sb_off, sb_out, w_gate_up, b_gate_up, w_down, b_down):
    _, d = h2.shape
    n_sb = sb_expert.shape[0]
    e, f_dim = w_down.shape[0], w_down.shape[1]
    n_f = f_dim // MOE_TILE
    n_o = d // MOE_TILE

    def f_tile(s, t, nv):
        return jnp.where(nv[s] > 0, jnp.minimum(t, n_f - 1), n_f - 1)

    def o_tile(s, t, nv):
        return jnp.where(nv[s] > 0, jnp.maximum(t - n_f, 0), n_o - 1)

    grid_spec = pltpu.PrefetchScalarGridSpec(
        num_scalar_prefetch=5,
        grid=(n_sb, n_f + n_o),
        in_specs=[
            pl.BlockSpec(memory_space=pl.ANY),
            pl.BlockSpec((None, d, MOE_TILE), lambda s, t, sr, ex, nv, of, ob: (ex[s], 0, f_tile(s, t, nv))),
            pl.BlockSpec((None, d, MOE_TILE), lambda s, t, sr, ex, nv, of, ob: (ex[s], 0, n_f + f_tile(s, t, nv))),
            pl.BlockSpec((None, 1, MOE_TILE), lambda s, t, sr, ex, nv, of, ob: (ex[s], 0, f_tile(s, t, nv))),
            pl.BlockSpec((None, 1, MOE_TILE), lambda s, t, sr, ex, nv, of, ob: (ex[s], 0, n_f + f_tile(s, t, nv))),
            pl.BlockSpec((None, f_dim, MOE_TILE), lambda s, t, sr, ex, nv, of, ob: (ex[s], 0, o_tile(s, t, nv))),
            pl.BlockSpec((None, 1, MOE_TILE), lambda s, t, sr, ex, nv, of, ob: (ex[s], 0, o_tile(s, t, nv))),
        ],
        out_specs=pl.BlockSpec(
            (MOE_ROWS, MOE_TILE),
            lambda s, t, sr, ex, nv, of, ob: (ob[s], jnp.where(nv[s] > 0, jnp.maximum(t - n_f, 0), 0))),
        scratch_shapes=[pltpu.VMEM((MOE_ROWS, d), BF16),
                        pltpu.VMEM((n_f, MOE_ROWS, MOE_TILE), BF16),
                        pltpu.VMEM((GATHER_SLOTS, GATHER_ROWS // F32_SUBLANES, F32_SUBLANES, d), F32),
                        pltpu.SemaphoreType.DMA((GATHER_SLOTS,))],
    )
    return pl.pallas_call(
        _moe_kernel,
        grid_spec=grid_spec,
        out_shape=jax.ShapeDtypeStruct(((n_sb + 1) * MOE_ROWS, d), F32),
        compiler_params=_cparams(("arbitrary", "arbitrary"), 58),
        name="moe",
    )(src_sorted, sb_expert, sb_rows, sb_off, sb_out, h2, w_gate_up, w_gate_up,
      b_gate_up.reshape(e, 1, -1), b_gate_up.reshape(e, 1, -1), w_down, b_down.reshape(e, 1, -1))


def _combine_kernel(dest_ref, x1_ref, gt_ref, gate_ref, gfin_ref, y_hbm, o_ref, buf, sem, *, tok_offset):
    i = pl.program_id(0)
    n_tiles = pl.num_programs(0)
    tm = x1_ref.shape[0]

    groups = tm // F32_SUBLANES

    def row_copy(tile, slot, g, u, kk):
        row = dest_ref[(tok_offset + tile * tm + g * F32_SUBLANES + u) * TOP_K + kk]
        return pltpu.make_async_copy(y_hbm.at[pl.ds(row, 1)], buf.at[slot, kk, g, pl.ds(u, 1)], sem.at[slot])

    def issue(tile, slot):
        def group(g, c):
            for u in range(F32_SUBLANES):
                for kk in range(TOP_K):
                    row_copy(tile, slot, g, u, kk).start()
            return c
        lax.fori_loop(0, groups, group, 0)

    @pl.when(i == 0)
    def _():
        issue(0, 0)

    @pl.when(i + 1 < n_tiles)
    def _():
        issue(i + 1, (i + 1) % 2)

    slot = i % 2

    def wait_group(g, c):
        for u in range(F32_SUBLANES):
            for kk in range(TOP_K):
                row_copy(i, slot, g, u, kk).wait()
        return c

    lax.fori_loop(0, groups, wait_group, 0)
    gate = gate_ref[...]
    y = jnp.zeros(x1_ref.shape, F32)
    for kk in range(TOP_K):
        y = y + gate[:, kk:kk + 1] * buf[slot, kk].reshape(x1_ref.shape)
    x2 = x1_ref[...] + gt_ref[...] * y
    o_ref[...] = _rms(x2, gfin_ref[...])


def _combine(x1, mod3, y_rows, dest, gates, g_final, tm, tok_offset):
    t, d = x1.shape
    g_cnt, r, _ = mod3.shape
    tiles_per_group = (t // tm) // g_cnt
    full = pl.BlockSpec((tm, d), lambda i, ds: (i, 0))
    grid_spec = pltpu.PrefetchScalarGridSpec(
        num_scalar_prefetch=1,
        grid=(t // tm,),
        in_specs=[full,
                  pl.BlockSpec((None, r, d), lambda i, ds: (i // tiles_per_group, 0, 5)),
                  pl.BlockSpec((tm, LANES), lambda i, ds: (i, 0)),
                  pl.BlockSpec((1, d), lambda i, ds: (0, 0)),
                  pl.BlockSpec(memory_space=pl.ANY)],
        out_specs=full,
        scratch_shapes=[pltpu.VMEM((2, TOP_K, tm // F32_SUBLANES, F32_SUBLANES, d), F32),
                        pltpu.SemaphoreType.DMA((2,))],
    )
    return pl.pallas_call(
        functools.partial(_combine_kernel, tok_offset=tok_offset),
        grid_spec=grid_spec,
        out_shape=jax.ShapeDtypeStruct((t, d), F32),
        compiler_params=_cparams(("arbitrary",), 40),
        name="combine",
    )(dest, x1, mod3, gates, g_final.reshape(1, d), y_rows)


def _route(top_idx):
    n = top_idx.shape[0]
    n_assign = n * TOP_K
    n_sb = n_assign // MOE_ROWS + N_EXPERTS
    e_flat = top_idx.reshape(-1)
    onehot = (e_flat[:, None] == jnp.arange(N_EXPERTS)[None, :]).astype(I32)
    csum = jnp.cumsum(onehot, axis=0)
    counts = csum[-1]
    rank = jnp.take_along_axis(csum, e_flat[:, None], axis=1)[:, 0] - 1
    seg_start = jnp.cumsum(counts) - counts
    bits = max(n_assign - 1, 1).bit_length()
    assert N_EXPERTS << bits < 2 ** 31
    keys = jnp.sort(e_flat.astype(I32) * (1 << bits) + jnp.arange(n_assign, dtype=I32))
    src_sorted = ((keys & ((1 << bits) - 1)) // TOP_K).astype(I32)
    sb_per_e = (counts + MOE_ROWS - 1) // MOE_ROWS
    sb_end = jnp.cumsum(sb_per_e)
    sb_start = sb_end - sb_per_e
    n_used = sb_end[-1]
    dest = (sb_start[e_flat] * MOE_ROWS + rank).astype(I32)
    sb = jnp.arange(n_sb)
    sb_clamped = jnp.minimum(sb, n_used - 1)
    sb_expert = jnp.minimum(jnp.searchsorted(sb_end, sb_clamped, side='right'), N_EXPERTS - 1).astype(I32)
    local = sb_clamped - sb_start[sb_expert]
    sb_rows = jnp.where(sb < n_used, jnp.clip(counts[sb_expert] - local * MOE_ROWS, 0, MOE_ROWS), 0).astype(I32)
    sb_off = (seg_start[sb_expert] + local * MOE_ROWS).astype(I32)
    sb_out = jnp.concatenate([jnp.where(sb < n_used, sb, n_sb), (n_used - 1)[None]]).astype(I32)
    return dest, src_sorted, sb_expert, sb_rows, sb_off, sb_out


def kernel(x_prompt, x_sample, cache_k, cache_v, page_table, c_prompt, c_sample, rel_bias_table, w_ada, b_ada,
           g_mix, w_in, g_sgu, w_sp, b_sp, g_out_attn, g_out_sgu, w_out, g_ffn, w_router, b_router,
           w_gate_up, b_gate_up, w_down, b_down, g_final):
    batch, seq, d = x_prompt.shape
    dec_batch, dec_seq, _ = x_sample.shape
    depth = w_ada.shape[0]
    assert depth == 1 and d == D_MODEL and seq % MOBA_BLOCK == 0 and seq // MOBA_BLOCK <= LANES
    n_p = batch * seq
    n_s = dec_batch * dec_seq
    past_len = page_table.shape[1] * PAGE_SIZE
    assert past_len % MOBA_BLOCK == 0 and dec_seq <= SGU_CHUNK and SGU_CHUNK % dec_seq == 0
    width = N_HEADS * HEAD_DIM
    drop = lambda a: a.reshape(a.shape[1:])

    mod = _adaln(jnp.concatenate([c_prompt, c_sample], axis=0), drop(w_ada), drop(b_ada))
    mod_p = mod[:batch].reshape(batch, 1, 6 * d)
    tm_s = min(n_s, 256)
    assert n_s % tm_s == 0 and tm_s % SGU_CHUNK == 0
    mod_s = jnp.repeat(mod[batch:], dec_seq, axis=0).reshape(n_s // tm_s, tm_s, 6 * d)

    w_in_bf = drop(w_in).astype(BF16)
    w_out_bf = drop(w_out).astype(BF16)
    xp = x_prompt.reshape(n_p, d)
    xs = x_sample.reshape(n_s, d)
    g_sgu_flat = g_sgu.reshape(-1)
    g_mix1, g_ffn1, g_oa, g_os = drop(g_mix), drop(g_ffn), drop(g_out_attn), drop(g_out_sgu)
    w_sp1, b_sp1 = drop(w_sp), drop(b_sp)

    q_p, k_p, v_p, u_p, vs_p = _inproj(xp, mod_p, g_mix1, w_in_bf, g_sgu_flat, 512)
    q_s, k_s, v_s, u_s, vs_s = _inproj(xs, mod_s, g_mix1, w_in_bf, g_sgu_flat, tm_s)

    attn_p = _attn_prompt(q_p, k_p, v_p, _bias_prompt(rel_bias_table), batch, seq)
    qh_rows = dec_seq * N_HEADS
    assert qh_rows <= LANES
    per_head = lambda a: a.reshape(dec_batch, qh_rows, HEAD_DIM)
    pad_new = lambda a: jnp.pad(per_head(a), ((0, 0), (0, LANES - qh_rows), (0, 0)))
    page_rows = PAGE_SIZE * N_HEADS
    attn_s = _attn_sample(per_head(q_s), pad_new(k_s), pad_new(v_s),
                          cache_k.reshape(-1, page_rows, HEAD_DIM), cache_v.reshape(-1, page_rows, HEAD_DIM),
                          page_table, _bias_sample(rel_bias_table, past_len, dec_seq)).reshape(n_s, width)

    tri = jnp.tril(jnp.ones((SGU_CHUNK, SGU_CHUNK), F32))
    reps = SGU_CHUNK // dec_seq
    seq_id = jnp.arange(SGU_CHUNK) // dec_seq
    mask_s = tri * (seq_id[:, None] == seq_id[None, :]).astype(F32)
    wmix_s = jnp.tile(w_sp1[:, :dec_seq, :dec_seq], (1, reps, reps))
    bmix_p = jnp.repeat(b_sp1.T, SGU_WIDTH // N_GROUPS, axis=1)
    bmix_s = jnp.repeat(jnp.tile(b_sp1[:, :dec_seq], (1, reps)).T, SGU_WIDTH // N_GROUPS, axis=1)
    wr_pad = jnp.pad(drop(w_router), ((0, 0), (0, LANES - N_EXPERTS))).astype(BF16)
    br_pad = jnp.pad(drop(b_router), (0, LANES - N_EXPERTS), constant_values=NEG).reshape(1, LANES)

    assert n_p % tm_s == 0
    x1_p, h2_part, ti_p, tg_p = _mixout(attn_p, u_p, vs_p, xp, mod_p, w_sp1, tri, bmix_p, g_oa,
                                        g_os, w_out_bf, g_ffn1, wr_pad, br_pad, 256, n_p + n_s, 0)
    x1_s, h2, ti_s, tg_s = _mixout(attn_s, u_s, vs_s, xs, mod_s, wmix_s, mask_s, bmix_s, g_oa,
                                   g_os, w_out_bf, g_ffn1, wr_pad, br_pad, tm_s, n_p + n_s, n_p, h2_part)
    top_idx = jnp.concatenate([ti_p, ti_s], axis=0)[:, :TOP_K]
    dest, src_sorted, sb_expert, sb_rows, sb_off, sb_out = _route(top_idx)
    y_rows = _moe(h2, src_sorted, sb_expert, sb_rows, sb_off, sb_out,
                  drop(w_gate_up), drop(b_gate_up), drop(w_down), drop(b_down))

    y_p = _combine(x1_p, mod_p, y_rows, dest, tg_p, g_final, 256, 0)
    y_s = _combine(x1_s, mod_s, y_rows, dest, tg_s, g_final, tm_s, n_p)

    hd = (N_HEADS, HEAD_DIM)
    gd = (N_GROUPS, SGU_WIDTH // N_GROUPS)
    last_chunk = ((seq - 1) // SGU_CHUNK) * SGU_CHUNK
    return (y_p.reshape(batch, seq, d), y_s.reshape(dec_batch, dec_seq, d),
            k_p.reshape(1, batch, seq, *hd), v_p.reshape(1, batch, seq, *hd),
            k_s.reshape(1, dec_batch, dec_seq, *hd), v_s.reshape(1, dec_batch, dec_seq, *hd),
            vs_p.reshape(batch, seq, *gd)[:, last_chunk:][None], vs_s.reshape(1, dec_batch, dec_seq, *gd))
```

```python
import functools
import math

import numpy as np
import jax
import jax.numpy as jnp
from jax import lax
from jax.experimental import pallas as pl
from jax.experimental.pallas import tpu as pltpu

F32 = jnp.float32
BF16 = jnp.bfloat16
I32 = jnp.int32

D_MODEL = 2048
ATTN_WIDTH = 1024
SGU_WIDTH = 1024
HEAD_DIM = 128
N_HEADS = 8
N_GROUPS = 8
MOBA_BLOCK = 256
MOBA_TOPK = 3
SGU_CHUNK = 128
N_BUCKETS = 32
MAX_DISTANCE = 128
N_EXPERTS = 32
TOP_K = 4
SWIGLU_ALPHA = 1.702
SWIGLU_LIMIT = 7.0
RMS_EPS = 1e-5
NEG = -1e30
PICKED = -3e38
PAGE_SIZE = 128
LANES = 128
MASKED_BUCKET = N_BUCKETS

MOE_ROWS = 3072
MOE_CHUNK = 128
MOE_TILE = 256
GATHER_ROWS = 128
GATHER_SLOTS = 6
F32_SUBLANES = 8
BF16_SUBLANES = 16


def _cparams(sem, vmem_mb):
    return pltpu.CompilerParams(dimension_semantics=sem, vmem_limit_bytes=vmem_mb << 20)


def _rms(x, g):
    return x * lax.rsqrt(jnp.mean(x * x, axis=-1, keepdims=True) + RMS_EPS) * g


def _gelu(x):
    return 0.5 * x * (1.0 + lax.erf(x * (2.0 ** -0.5)))


def _adaln_kernel(c_ref, w_ref, b_ref, o_ref):
    c = c_ref[...]
    a = (c * jax.nn.sigmoid(c)).astype(BF16)
    o_ref[...] = jnp.dot(a, w_ref[...].astype(BF16), preferred_element_type=F32) + b_ref[...]


def _adaln(c, w_ada, b_ada):
    r, d = c.shape
    n = w_ada.shape[1]
    tn = 1024
    return pl.pallas_call(
        _adaln_kernel,
        grid=(n // tn,),
        in_specs=[pl.BlockSpec((r, d), lambda i: (0, 0)),
                  pl.BlockSpec((d, tn), lambda i: (0, i)),
                  pl.BlockSpec((1, tn), lambda i: (0, i))],
        out_specs=pl.BlockSpec((r, tn), lambda i: (0, i)),
        out_shape=jax.ShapeDtypeStruct((r, n), F32),
        compiler_params=_cparams(("arbitrary",), 40),
        name="adaln",
    )(c, w_ada, b_ada.reshape(1, n))


def _inproj_kernel(x_ref, sh_ref, sc_ref, gmix_ref, w_ref, gsgu_ref,
                   q_ref, k_ref, v_ref, u_ref, vs_ref, h_sc):
    n = pl.program_id(1)

    @pl.when(n == 0)
    def _():
        h = _rms(x_ref[...], gmix_ref[...]) * (1.0 + sc_ref[...]) + sh_ref[...]
        h_sc[...] = h.astype(BF16)

    def proj(cols):
        return jnp.dot(h_sc[...], w_ref[:, cols], preferred_element_type=F32)

    for idx, out_ref in enumerate((q_ref, k_ref, v_ref)):
        @pl.when(n == idx)
        def _(out_ref=out_ref):
            out_ref[...] = proj(slice(None))

    slab = 2 * LANES

    @pl.when(n == 3)
    def _():
        for c in range(0, ATTN_WIDTH, slab):
            u_ref[:, c:c + slab] = _gelu(proj(slice(c, c + slab)))

    @pl.when(n == 4)
    def _():
        for c in range(0, ATTN_WIDTH, slab):
            g = _gelu(proj(slice(c, c + slab)))
            for i in range(slab // LANES):
                sl = slice(c + i * LANES, c + (i + 1) * LANES)
                vs_ref[:, sl] = _rms(g[:, i * LANES:(i + 1) * LANES], gsgu_ref[:, sl])


def _inproj(x, mod3, g_mix, w_in_bf, g_sgu, tm):
    t, d = x.shape
    g_cnt, r, _ = mod3.shape
    tiles_per_group = (t // tm) // g_cnt
    w = ATTN_WIDTH
    mod_spec = lambda chunk: pl.BlockSpec((None, r, d), lambda i, n: (i // tiles_per_group, 0, chunk))
    out_spec = pl.BlockSpec((tm, w), lambda i, n: (i, 0))
    out_sds = jax.ShapeDtypeStruct((t, w), F32)
    return pl.pallas_call(
        _inproj_kernel,
        grid=(t // tm, 5),
        in_specs=[pl.BlockSpec((tm, d), lambda i, n: (i, 0)),
                  mod_spec(0), mod_spec(1),
                  pl.BlockSpec((1, d), lambda i, n: (0, 0)),
                  pl.BlockSpec((d, w), lambda i, n: (0, n)),
                  pl.BlockSpec((1, w), lambda i, n: (0, 0))],
        out_specs=[out_spec] * 5,
        out_shape=[out_sds] * 5,
        scratch_shapes=[pltpu.VMEM((tm, d), BF16)],
        compiler_params=_cparams(("parallel", "arbitrary"), 56),
        name="inproj",
    )(x, mod3, mod3, g_mix.reshape(1, d), w_in_bf, g_sgu.reshape(1, w))


def _bucket_np(dist):
    dist = np.maximum(dist, 0)
    max_exact = N_BUCKETS // 2
    dist_f = np.maximum(dist, max_exact).astype(np.float32)
    large = max_exact + (np.log(dist_f / np.float32(max_exact)) / np.float32(math.log(MAX_DISTANCE / max_exact))
                         * np.float32(N_BUCKETS - max_exact)).astype(np.int32)
    return np.where(dist < max_exact, dist, np.minimum(large, N_BUCKETS - 1)).astype(np.int32)


def _bias_prompt_kernel(tab_ref, bucket_ref, o_ref):
    h = pl.program_id(0)
    b = bucket_ref[...]
    acc = jnp.zeros(b.shape, F32)
    for kk in range(N_BUCKETS):
        acc = jnp.where(b == kk, tab_ref[kk, h], acc)
    o_ref[0] = jnp.where(b == MASKED_BUCKET, NEG, acc)


def _bias_prompt(table):
    r = np.arange(MOBA_BLOCK)[:, None]
    c = np.arange(MOBA_BLOCK)[None, :]
    own = np.where(c <= r, _bucket_np(r - c), MASKED_BUCKET)
    prev = _bucket_np(MOBA_BLOCK + r - c)
    far = _bucket_np(2 * MOBA_BLOCK + r - c)
    assert (far == far[0, 0]).all()
    buckets = jnp.asarray(np.stack([own, prev, far]).astype(np.int32))
    return pl.pallas_call(
        _bias_prompt_kernel,
        grid=(N_HEADS,),
        in_specs=[pl.BlockSpec(memory_space=pltpu.SMEM),
                  pl.BlockSpec((3, MOBA_BLOCK, MOBA_BLOCK), lambda h: (0, 0, 0))],
        out_specs=pl.BlockSpec((1, 3, MOBA_BLOCK, MOBA_BLOCK), lambda h: (h, 0, 0, 0)),
        out_shape=jax.ShapeDtypeStruct((N_HEADS, 3, MOBA_BLOCK, MOBA_BLOCK), F32),
        compiler_params=_cparams(("arbitrary",), 16),
        name="bias_prompt",
    )(table, buckets)


def _bias_sample_kernel(tabt_ref, bucket_ref, o_ref):
    b = bucket_ref[...]
    acc = jnp.zeros(b.shape, F32)
    for kk in range(N_BUCKETS):
        acc = jnp.where(b == kk, tabt_ref[:, kk:kk + 1], acc)
    o_ref[...] = jnp.where(b == MASKED_BUCKET, NEG, acc)


def _bias_sample(table, past_len, dec_seq):
    rows = dec_seq * N_HEADS
    qi = (np.arange(rows) // N_HEADS)[:, None]
    qh = (np.arange(rows) % N_HEADS)[:, None]
    col = np.arange(past_len * N_HEADS)[None, :]
    past = np.where(col % N_HEADS == qh, _bucket_np(past_len + qi - col // N_HEADS), MASKED_BUCKET)
    col = np.arange(LANES)[None, :]
    own_ok = (col % N_HEADS == qh) & (col // N_HEADS <= qi)
    own = np.where(own_ok, _bucket_np(qi - col // N_HEADS), MASKED_BUCKET)
    buckets = jnp.asarray(np.concatenate([past, own], axis=1).astype(np.int32))
    tab_t = jnp.tile(table, (1, dec_seq)).T
    return pl.pallas_call(
        _bias_sample_kernel,
        out_shape=jax.ShapeDtypeStruct(buckets.shape, F32),
        name="bias_sample",
    )(tab_t, buckets)


def _attn_prompt_kernel(q_ref, k_ref, v_ref, bias_ref, o_ref, ka_sc, vb_sc, km_sc, s_sc):
    blk = MOBA_BLOCK
    seq = k_ref.shape[0]
    nb = seq // blk
    mrows = km_sc.shape[0]
    scale = HEAD_DIM ** -0.5
    nt = (((1,), (1,)), ((), ()))

    k = k_ref[...]
    ka_sc[:, :HEAD_DIM] = k.astype(BF16)
    row_blk = lax.broadcasted_iota(I32, (seq, LANES), 0) // blk
    ka_sc[:, HEAD_DIM:] = (lax.broadcasted_iota(I32, (seq, LANES), 1) == row_blk).astype(BF16)
    vb_sc[...] = v_ref[...].astype(BF16)
    means = [jnp.mean(k[n * blk:(n + 1) * blk, :], axis=0, keepdims=True) for n in range(nb)]
    means.append(jnp.zeros((mrows - nb, HEAD_DIM), F32))
    km_sc[...] = jnp.concatenate(means, axis=0).astype(BF16)
    sub = lax.broadcasted_iota(I32, (mrows, blk), 0).astype(F32)

    def rows(n):
        return slice(n * blk, (n + 1) * blk)

    slot = 0
    for j in range(nb):
        qb = q_ref[rows(j), :].astype(BF16)
        if j > MOBA_TOPK:
            gates = lax.dot_general(km_sc[...], qb, nt, preferred_element_type=F32)
            gates = jnp.where(sub < float(j), gates, NEG)
            allow = jnp.full((mrows, blk), NEG, F32)
            for _ in range(MOBA_TOPK):
                best = jnp.max(gates, axis=0, keepdims=True)
                first = jnp.min(jnp.where(gates == best, sub, float(mrows)), axis=0, keepdims=True)
                hit = sub == first
                allow = jnp.where(hit, 0.0, allow)
                gates = jnp.where(hit, PICKED, gates)
            allow = jnp.concatenate([allow, jnp.full((LANES - mrows, blk), NEG, F32)], axis=0).T
            q_past = jnp.concatenate([qb, allow.astype(BF16)], axis=1)
            k_cols = slice(0, HEAD_DIM + LANES)
        else:
            q_past = qb
            k_cols = slice(0, HEAD_DIM)

        s_own = lax.dot_general(qb, ka_sc[rows(j), :HEAD_DIM], nt, preferred_element_type=F32) * scale
        s_own = s_own + bias_ref[0, 0]
        run = s_own
        base = slot
        for n in range(j):
            s = lax.dot_general(q_past, ka_sc[rows(n), k_cols], nt, preferred_element_type=F32) * scale
            s = s + bias_ref[0, min(j - n, 2)]
            s_sc[base + n] = s
            run = jnp.maximum(run, s)
        slot += j
        m = jnp.max(run, axis=-1, keepdims=True)
        p = jnp.exp(s_own - m)
        psum = p
        acc = jnp.dot(p.astype(BF16), vb_sc[rows(j), :], preferred_element_type=F32)
        for n in range(j):
            p = jnp.exp(s_sc[base + n] - m)
            psum = psum + p
            acc = acc + jnp.dot(p.astype(BF16), vb_sc[rows(n), :], preferred_element_type=F32)
        o_ref[rows(j), :] = acc / jnp.sum(psum, axis=-1, keepdims=True)


def _attn_prompt(q, k, v, bias_tiles, batch, seq):
    nb = seq // MOBA_BLOCK
    spec = pl.BlockSpec((seq, HEAD_DIM), lambda b, h: (b, h))
    return pl.pallas_call(
        _attn_prompt_kernel,
        grid=(batch, N_HEADS),
        in_specs=[spec, spec, spec,
                  pl.BlockSpec((1, 3, MOBA_BLOCK, MOBA_BLOCK), lambda b, h: (h, 0, 0, 0))],
        out_specs=spec,
        out_shape=jax.ShapeDtypeStruct(q.shape, F32),
        scratch_shapes=[pltpu.VMEM((seq, HEAD_DIM + LANES), BF16), pltpu.VMEM((seq, HEAD_DIM), BF16),
                        pltpu.VMEM((pl.cdiv(nb, BF16_SUBLANES) * BF16_SUBLANES, HEAD_DIM), BF16),
                        pltpu.VMEM((max(nb * (nb - 1) // 2, 1), MOBA_BLOCK, MOBA_BLOCK), F32)],
        compiler_params=_cparams(("parallel", "parallel"), 40),
        name="attn_prompt",
    )(q, k, v, bias_tiles)


def _attn_sample_kernel(pt_ref, q_ref, kn_ref, vn_ref, bias_ref, ck_hbm, cv_hbm, o_ref,
                        pages_sc, s_sc, sem, *, n_pages):
    i = pl.program_id(0)
    s = pl.program_id(1)
    n_seq = pl.num_programs(0)
    rows = q_ref.shape[0]
    cols = PAGE_SIZE * N_HEADS
    n_blocks = n_pages * PAGE_SIZE // MOBA_BLOCK
    ppb = MOBA_BLOCK // PAGE_SIZE
    scale = HEAD_DIM ** -0.5
    nt = (((1,), (1,)), ((), ()))

    def page_copy(seq, kind, pg):
        cache = cv_hbm if kind else ck_hbm
        return pltpu.make_async_copy(cache.at[pt_ref[seq * n_pages + pg]], pages_sc.at[kind, pg], sem.at[kind])

    def start_pages(seq, kind):
        for pg in range(n_pages):
            page_copy(seq, kind, pg).start()

    def wait_pages(seq, kind):
        for pg in range(n_pages):
            page_copy(seq, kind, pg).wait()

    @pl.when((i == 0) & (s == 0))
    def _():
        start_pages(0, 0)

    @pl.when(s == 0)
    def _():
        start_pages(i, 1)
        wait_pages(i, 0)
        qb = q_ref[...].astype(BF16)
        for pg in range(n_pages):
            s_sc[pg] = lax.dot_general(qb, pages_sc[0, pg].astype(BF16), nt, preferred_element_type=F32)

    @pl.when((s == 1) & (i + 1 < n_seq))
    def _():
        start_pages(i + 1, 0)

    @pl.when(s == 1)
    def _():
        same_head = (lax.broadcasted_iota(I32, (rows, cols), 1) % N_HEADS
                     == lax.broadcasted_iota(I32, (rows, cols), 0) % N_HEADS)
        gates = []
        for n in range(n_blocks):
            tot = s_sc[n * ppb]
            for t in range(1, ppb):
                tot = tot + s_sc[n * ppb + t]
            gates.append(jnp.sum(jnp.where(same_head, tot, 0.0), axis=-1, keepdims=True))
        allow = []
        for n in range(n_blocks):
            ahead = jnp.zeros((rows, 1), F32)
            for mm in range(n_blocks):
                if mm == n:
                    continue
                before = (gates[mm] >= gates[n]) if mm < n else (gates[mm] > gates[n])
                ahead = ahead + before.astype(F32)
            allow.append(jnp.where(ahead < MOBA_TOPK, 0.0, NEG))
        qb = q_ref[...].astype(BF16)
        past_cols = n_pages * cols
        own = lax.dot_general(qb, kn_ref[...].astype(BF16), nt, preferred_element_type=F32) * scale
        own = own + bias_ref[:, past_cols:]
        run = jnp.full((rows, cols), NEG, F32)
        for t in range(n_pages):
            lg = s_sc[t] * scale + bias_ref[:, t * cols:(t + 1) * cols] + allow[t // ppb]
            s_sc[t] = lg
            run = jnp.maximum(run, lg)
        m = jnp.maximum(jnp.max(own, axis=-1, keepdims=True), jnp.max(run, axis=-1, keepdims=True))
        p_own = jnp.exp(own - m)
        psum = jnp.zeros((rows, cols), F32)
        for t in range(n_pages):
            p = jnp.exp(s_sc[t] - m)
            s_sc[t] = p
            psum = psum + p
        den = jnp.sum(p_own, axis=-1, keepdims=True) + jnp.sum(psum, axis=-1, keepdims=True)
        acc = jnp.dot(p_own.astype(BF16), vn_ref[...].astype(BF16), preferred_element_type=F32)
        wait_pages(i, 1)
        for pg in range(n_pages):
            acc = acc + jnp.dot(s_sc[pg].astype(BF16), pages_sc[1, pg].astype(BF16), preferred_element_type=F32)
        o_ref[...] = acc / den


def _attn_sample(q, k_new, v_new, cache_k, cache_v, page_table, bias_s):
    b, rows, hd = q.shape
    n_pages = page_table.shape[1]
    cols = PAGE_SIZE * N_HEADS
    q_spec = pl.BlockSpec((None, rows, hd), lambda i, s, pt: (i, 0, 0))
    new_spec = pl.BlockSpec((None, LANES, hd), lambda i, s, pt: (i, 0, 0))
    grid_spec = pltpu.PrefetchScalarGridSpec(
        num_scalar_prefetch=1,
        grid=(b, 2),
        in_specs=[q_spec, new_spec, new_spec,
                  pl.BlockSpec(bias_s.shape, lambda i, s, pt: (0, 0)),
                  pl.BlockSpec(memory_space=pl.ANY), pl.BlockSpec(memory_space=pl.ANY)],
        out_specs=q_spec,
        scratch_shapes=[pltpu.VMEM((2, n_pages, cols, hd), F32),
                        pltpu.VMEM((n_pages, rows, cols), F32),
                        pltpu.SemaphoreType.DMA((2,))],
    )
    return pl.pallas_call(
        functools.partial(_attn_sample_kernel, n_pages=n_pages),
        grid_spec=grid_spec,
        out_shape=jax.ShapeDtypeStruct(q.shape, F32),
        compiler_params=_cparams(("arbitrary", "arbitrary"), 48),
        name="attn_sample",
    )(page_table.reshape(-1), q, k_new, v_new, bias_s, cache_k, cache_v)


def _mixout_kernel(attn_ref, u_ref, vs_ref, x_ref, gt_ref, sc_ref, sh_ref, wmix_ref, mask_ref, bmix_ref,
                   ga_ref, gs_ref, wout_ref, gffn_ref, wr_ref, br_ref, *rest):
    x1_ref, h2_ref, topi_ref, topg_ref, sgu_sc = rest[-5:]
    tm = x_ref.shape[0]
    for g in range(N_GROUPS):
        cols = slice(g * LANES, (g + 1) * LANES)
        wg = (wmix_ref[g] * mask_ref[...]).astype(BF16)
        for c in range(tm // SGU_CHUNK):
            rws = slice(c * SGU_CHUNK, (c + 1) * SGU_CHUNK)
            mixed = jnp.dot(wg, vs_ref[rws, cols].astype(BF16), preferred_element_type=F32) + bmix_ref[:, cols]
            sgu_sc[rws, cols] = u_ref[rws, cols] * mixed
    an = _rms(attn_ref[...], ga_ref[...]).astype(BF16)
    sn = _rms(sgu_sc[...], gs_ref[...]).astype(BF16)
    out = (jnp.dot(an, wout_ref[:ATTN_WIDTH, :], preferred_element_type=F32)
           + jnp.dot(sn, wout_ref[ATTN_WIDTH:, :], preferred_element_type=F32))
    x1 = x_ref[...] + gt_ref[...] * out
    x1_ref[...] = x1
    h2 = _rms(x1, gffn_ref[...]) * (1.0 + sc_ref[...]) + sh_ref[...]
    h2_ref[...] = h2
    logits = jnp.dot(h2.astype(BF16), wr_ref[...], preferred_element_type=F32) + br_ref[...]
    lane = lax.broadcasted_iota(I32, logits.shape, 1).astype(F32)
    idx_out = jnp.zeros(logits.shape, F32)
    val_out = jnp.zeros(logits.shape, F32)
    top = None
    den = None
    for r in range(TOP_K):
        best = jnp.max(logits, axis=-1, keepdims=True)
        first = jnp.min(jnp.where(logits == best, lane, float(LANES)), axis=-1, keepdims=True)
        if r == 0:
            top = best
        e = jnp.exp(best - top)
        den = e if r == 0 else den + e
        idx_out = jnp.where(lane == float(r), first, idx_out)
        val_out = jnp.where(lane == float(r), e, val_out)
        logits = jnp.where(lane == first, PICKED, logits)
    topi_ref[...] = idx_out.astype(I32)
    topg_ref[...] = val_out / den


def _mixout(attn, u, vs, x, mod3, wmix, mixmask, bmix, g_out_attn, g_out_sgu, w_out_bf, g_ffn, wr_pad, br_pad, tm,
            h2_rows, h2_row0, h2_shared=None):
    t, d = x.shape
    g_cnt, r, _ = mod3.shape
    tiles_per_group = (t // tm) // g_cnt
    w = ATTN_WIDTH
    assert h2_row0 % tm == 0
    mod_spec = lambda chunk: pl.BlockSpec((None, r, d), lambda i: (i // tiles_per_group, 0, chunk))
    half = pl.BlockSpec((tm, w), lambda i: (i, 0))
    full = pl.BlockSpec((tm, d), lambda i: (i, 0))
    small = pl.BlockSpec((tm, LANES), lambda i: (i, 0))
    const = lambda shape: pl.BlockSpec(shape, lambda i: (0,) * len(shape))
    in_specs = [half, half, half, full, mod_spec(2), mod_spec(4), mod_spec(3),
                const(wmix.shape), const(mixmask.shape), const(bmix.shape),
                const((1, w)), const((1, w)), const((d, d)), const((1, d)),
                const((d, LANES)), const((1, LANES))]
    args = [attn, u, vs, x, mod3, mod3, mod3, wmix, mixmask, bmix, g_out_attn.reshape(1, w), g_out_sgu.reshape(1, w),
            w_out_bf, g_ffn.reshape(1, d), wr_pad, br_pad]
    aliases = {}
    if h2_shared is not None:
        aliases = {len(args): 1}
        in_specs.append(pl.BlockSpec(memory_space=pl.ANY))
        args.append(h2_shared)
    return pl.pallas_call(
        _mixout_kernel,
        grid=(t // tm,),
        in_specs=in_specs,
        out_specs=[full, pl.BlockSpec((tm, d), lambda i: (i + h2_row0 // tm, 0)), small, small],
        out_shape=[jax.ShapeDtypeStruct((t, d), F32), jax.ShapeDtypeStruct((h2_rows, d), F32),
                   jax.ShapeDtypeStruct((t, LANES), I32), jax.ShapeDtypeStruct((t, LANES), F32)],
        scratch_shapes=[pltpu.VMEM((tm, w), F32)],
        input_output_aliases=aliases,
        compiler_params=_cparams(("parallel",), 48),
        name="mixout",
    )(*args)


def _moe_kernel(src_ref, exp_ref, nv_ref, off_ref, oblk_ref, h_hbm, wg_ref, wu_ref, bg_ref, bu_ref, wd_ref, bd_ref,
                y_ref, x_sc, act_sc, stage_sc, sem):
    del exp_ref, oblk_ref
    sb = pl.program_id(0)
    t = pl.program_id(1)
    n_f = act_sc.shape[0]
    nv = nv_ref[sb]
    n_chunks = (nv + MOE_CHUNK - 1) // MOE_CHUNK

    @pl.when((t == 0) & (nv > 0))
    def _():
        off = off_ref[sb]
        n_batches = n_chunks * (MOE_CHUNK // GATHER_ROWS)

        def row_copy(bt, slot, i, u):
            tok = src_ref[off + bt * GATHER_ROWS + i * F32_SUBLANES + u]
            return pltpu.make_async_copy(h_hbm.at[pl.ds(tok, 1)], stage_sc.at[slot, i, pl.ds(u, 1)], sem.at[slot])

        def issue(bt, slot):
            def group(i, c):
                for u in range(F32_SUBLANES):
                    row_copy(bt, slot, i, u).start()
                return c
            lax.fori_loop(0, GATHER_ROWS // F32_SUBLANES, group, 0)

        def drain(bt, slot):
            def group(i, c):
                for u in range(F32_SUBLANES):
                    row_copy(bt, slot, i, u).wait()
                return c
            lax.fori_loop(0, GATHER_ROWS // F32_SUBLANES, group, 0)
            rows = stage_sc[slot].reshape(GATHER_ROWS, x_sc.shape[1])
            x_sc[pl.ds(pl.multiple_of(bt * GATHER_ROWS, GATHER_ROWS), GATHER_ROWS), :] = rows.astype(BF16)

        n_slots = stage_sc.shape[0]
        ahead = n_slots - 1
        for bt in range(ahead):
            @pl.when(bt < n_batches)
            def _():
                issue(bt, bt)

        def body(bt, c):
            @pl.when(bt + ahead < n_batches)
            def _():
                issue(bt + ahead, (bt + ahead) % n_slots)

            drain(bt, bt % n_slots)
            return c

        lax.fori_loop(0, n_batches, body, 0)

    def cascade(fn):
        big = 16 * MOE_CHUNK
        n_big = n_chunks // 16

        def body(i, c):
            fn(pl.multiple_of(i * big, big), big)
            return c

        lax.fori_loop(0, n_big, body, 0)
        rem = n_chunks - n_big * 16
        base = n_big * big
        for part in (8, 4, 2, 1):
            done = (rem // (2 * part)) * (2 * part)

            @pl.when((rem // part) % 2 == 1)
            def _(part=part, done=done):
                fn(pl.multiple_of(base + done * MOE_CHUNK, MOE_CHUNK), part * MOE_CHUNK)

    @pl.when(t < n_f)
    def _():
        bg = bg_ref[...]
        bu = bu_ref[...]

        def up_proj(start, size):
            xb = x_sc[pl.ds(start, size), :]
            g = jnp.dot(xb, wg_ref[...].astype(BF16), preferred_element_type=F32) + bg
            up = jnp.dot(xb, wu_ref[...].astype(BF16), preferred_element_type=F32) + bu
            g = jnp.minimum(g, SWIGLU_LIMIT)
            up = jnp.clip(up, -SWIGLU_LIMIT, SWIGLU_LIMIT)
            act = g * jax.nn.sigmoid(SWIGLU_ALPHA * g) * (up + 1.0)
            act_sc[t, pl.ds(start, size), :] = act.astype(BF16)

        cascade(up_proj)

    @pl.when(t >= n_f)
    def _():
        bd = bd_ref[...]

        def down_proj(start, size):
            acc = jnp.zeros((size, MOE_TILE), F32) + bd
            for f in range(n_f):
                wd = wd_ref[f * MOE_TILE:(f + 1) * MOE_TILE, :].astype(BF16)
                acc = acc + jnp.dot(act_sc[f, pl.ds(start, size), :], wd, preferred_element_type=F32)
            y_ref[pl.ds(start, size), :] = acc

        cascade(down_proj)


def _moe(h2, src_sorted, sb_expert, sb_rows, sb_off, sb_out, w_gate_up, b_gate_up, w_down, b_down):
    _, d = h2.shape
    n_sb = sb_expert.shape[0]
    e, f_dim = w_down.shape[0], w_down.shape[1]
    n_f = f_dim // MOE_TILE
    n_o = d // MOE_TILE

    def f_tile(s, t, nv):
        return jnp.where(nv[s] > 0, jnp.minimum(t, n_f - 1), n_f - 1)

    def o_tile(s, t, nv):
        return jnp.where(nv[s] > 0, jnp.maximum(t - n_f, 0), n_o - 1)

    grid_spec = pltpu.PrefetchScalarGridSpec(
        num_scalar_prefetch=5,
        grid=(n_sb, n_f + n_o),
        in_specs=[
            pl.BlockSpec(memory_space=pl.ANY),
            pl.BlockSpec((None, d, MOE_TILE), lambda s, t, sr, ex, nv, of, ob: (ex[s], 0, f_tile(s, t, nv))),
            pl.BlockSpec((None, d, MOE_TILE), lambda s, t, sr, ex, nv, of, ob: (ex[s], 0, n_f + f_tile(s, t, nv))),
            pl.BlockSpec((None, 1, MOE_TILE), lambda s, t, sr, ex, nv, of, ob: (ex[s], 0, f_tile(s, t, nv))),
            pl.BlockSpec((None, 1, MOE_TILE), lambda s, t, sr, ex, nv, of, ob: (ex[s], 0, n_f + f_tile(s, t, nv))),
            pl.BlockSpec((None, f_dim, MOE_TILE), lambda s, t, sr, ex, nv, of, ob: (ex[s], 0, o_tile(s, t, nv))),
            pl.BlockSpec((None, 1, MOE_TILE), lambda s, t, sr, ex, nv, of, ob: (ex[s], 0, o_tile(s, t, nv))),
        ],
        out_specs=pl.BlockSpec(
            (MOE_ROWS, MOE_TILE),
            lambda s, t, sr, ex, nv, of, ob: (ob[s], jnp.where(nv[s] > 0, jnp.maximum(t - n_f, 0), 0))),
        scratch_shapes=[pltpu.VMEM((MOE_ROWS, d), BF16),
                        pltpu.VMEM((n_f, MOE_ROWS, MOE_TILE), BF16),
                        pltpu.VMEM((GATHER_SLOTS, GATHER_ROWS // F32_SUBLANES, F32_SUBLANES, d), F32),
                        pltpu.SemaphoreType.DMA((GATHER_SLOTS,))],
    )
    return pl.pallas_call(
        _moe_kernel,
        grid_spec=grid_spec,
        out_shape=jax.ShapeDtypeStruct(((n_sb + 1) * MOE_ROWS, d), F32),
        compiler_params=_cparams(("arbitrary", "arbitrary"), 58),
        name="moe",
    )(src_sorted, sb_expert, sb_rows, sb_off, sb_out, h2, w_gate_up, w_gate_up,
      b_gate_up.reshape(e, 1, -1), b_gate_up.reshape(e, 1, -1), w_down, b_down.reshape(e, 1, -1))


def _combine_kernel(dest_ref, x1_ref, gt_ref, gate_ref, gfin_ref, y_hbm, o_ref, buf, sem, *, tok_offset):
    i = pl.program_id(0)
    n_tiles = pl.num_programs(0)
    tm = x1_ref.shape[0]

    groups = tm // F32_SUBLANES

    def row_copy(tile, slot, g, u, kk):
        row = dest_ref[(tok_offset + tile * tm + g * F32_SUBLANES + u) * TOP_K + kk]
        return pltpu.make_async_copy(y_hbm.at[pl.ds(row, 1)], buf.at[slot, kk, g, pl.ds(u, 1)], sem.at[slot])

    def issue(tile, slot):
        def group(g, c):
            for u in range(F32_SUBLANES):
                for kk in range(TOP_K):
                    row_copy(tile, slot, g, u, kk).start()
            return c
        lax.fori_loop(0, groups, group, 0)

    @pl.when(i == 0)
    def _():
        issue(0, 0)

    @pl.when(i + 1 < n_tiles)
    def _():
        issue(i + 1, (i + 1) % 2)

    slot = i % 2

    def wait_group(g, c):
        for u in range(F32_SUBLANES):
            for kk in range(TOP_K):
                row_copy(i, slot, g, u, kk).wait()
        return c

    lax.fori_loop(0, groups, wait_group, 0)
    gate = gate_ref[...]
    y = jnp.zeros(x1_ref.shape, F32)
    for kk in range(TOP_K):
        y = y + gate[:, kk:kk + 1] * buf[slot, kk].reshape(x1_ref.shape)
    x2 = x1_ref[...] + gt_ref[...] * y
    o_ref[...] = _rms(x2, gfin_ref[...])


def _combine(x1, mod3, y_rows, dest, gates, g_final, tm, tok_offset):
    t, d = x1.shape
    g_cnt, r, _ = mod3.shape
    tiles_per_group = (t // tm) // g_cnt
    full = pl.BlockSpec((tm, d), lambda i, ds: (i, 0))
    grid_spec = pltpu.PrefetchScalarGridSpec(
        num_scalar_prefetch=1,
        grid=(t // tm,),
        in_specs=[full,
                  pl.BlockSpec((None, r, d), lambda i, ds: (i // tiles_per_group, 0, 5)),
                  pl.BlockSpec((tm, LANES), lambda i, ds: (i, 0)),
                  pl.BlockSpec((1, d), lambda i, ds: (0, 0)),
                  pl.BlockSpec(memory_space=pl.ANY)],
        out_specs=full,
        scratch_shapes=[pltpu.VMEM((2, TOP_K, tm // F32_SUBLANES, F32_SUBLANES, d), F32),
                        pltpu.SemaphoreType.DMA((2,))],
    )
    return pl.pallas_call(
        functools.partial(_combine_kernel, tok_offset=tok_offset),
        grid_spec=grid_spec,
        out_shape=jax.ShapeDtypeStruct((t, d), F32),
        compiler_params=_cparams(("arbitrary",), 40),
        name="combine",
    )(dest, x1, mod3, gates, g_final.reshape(1, d), y_rows)


def _route(top_idx):
    n = top_idx.shape[0]
    n_assign = n * TOP_K
    n_sb = n_assign // MOE_ROWS + N_EXPERTS
    e_flat = top_idx.reshape(-1)
    onehot = (e_flat[:, None] == jnp.arange(N_EXPERTS)[None, :]).astype(I32)
    csum = jnp.cumsum(onehot, axis=0)
    counts = csum[-1]
    rank = jnp.take_along_axis(csum, e_flat[:, None], axis=1)[:, 0] - 1
    seg_start = jnp.cumsum(counts) - counts
    bits = max(n_assign - 1, 1).bit_length()
    assert N_EXPERTS << bits < 2 ** 31
    keys = jnp.sort(e_flat.astype(I32) * (1 << bits) + jnp.arange(n_assign, dtype=I32))
    src_sorted = ((keys & ((1 << bits) - 1)) // TOP_K).astype(I32)
    src_sorted = jnp.pad(src_sorted, (0, MOE_CHUNK))
    sb_per_e = (counts + MOE_ROWS - 1) // MOE_ROWS
    sb_end = jnp.cumsum(sb_per_e)
    sb_start = sb_end - sb_per_e
    n_used = sb_end[-1]
    dest = (sb_start[e_flat] * MOE_ROWS + rank).astype(I32)
    sb = jnp.arange(n_sb)
    sb_clamped = jnp.minimum(sb, n_used - 1)
    sb_expert = jnp.minimum(jnp.searchsorted(sb_end, sb_clamped, side='right'), N_EXPERTS - 1).astype(I32)
    local = sb_clamped - sb_start[sb_expert]
    sb_rows = jnp.where(sb < n_used, jnp.clip(counts[sb_expert] - local * MOE_ROWS, 0, MOE_ROWS), 0).astype(I32)
    sb_off = (seg_start[sb_expert] + local * MOE_ROWS).astype(I32)
    sb_out = jnp.concatenate([jnp.where(sb < n_used, sb, n_sb), (n_used - 1)[None]]).astype(I32)
    return dest, src_sorted, sb_expert, sb_rows, sb_off, sb_out


def kernel(x_prompt, x_sample, cache_k, cache_v, page_table, c_prompt, c_sample, rel_bias_table, w_ada, b_ada,
           g_mix, w_in, g_sgu, w_sp, b_sp, g_out_attn, g_out_sgu, w_out, g_ffn, w_router, b_router,
           w_gate_up, b_gate_up, w_down, b_down, g_final):
    batch, seq, d = x_prompt.shape
    dec_batch, dec_seq, _ = x_sample.shape
    depth = w_ada.shape[0]
    assert depth == 1 and d == D_MODEL and seq % MOBA_BLOCK == 0 and seq // MOBA_BLOCK <= LANES
    n_p = batch * seq
    n_s = dec_batch * dec_seq
    past_len = page_table.shape[1] * PAGE_SIZE
    assert past_len % MOBA_BLOCK == 0 and dec_seq <= SGU_CHUNK and SGU_CHUNK % dec_seq == 0
    width = N_HEADS * HEAD_DIM
    drop = lambda a: a.reshape(a.shape[1:])

    mod = _adaln(jnp.concatenate([c_prompt, c_sample], axis=0), drop(w_ada), drop(b_ada))
    mod_p = mod[:batch].reshape(batch, 1, 6 * d)
    tm_s = min(n_s, 256)
    assert n_s % tm_s == 0 and tm_s % SGU_CHUNK == 0
    mod_s = jnp.repeat(mod[batch:], dec_seq, axis=0).reshape(n_s // tm_s, tm_s, 6 * d)

    w_in_bf = drop(w_in).astype(BF16)
    w_out_bf = drop(w_out).astype(BF16)
    xp = x_prompt.reshape(n_p, d)
    xs = x_sample.reshape(n_s, d)
    g_sgu_flat = g_sgu.reshape(-1)
    g_mix1, g_ffn1, g_oa, g_os = drop(g_mix), drop(g_ffn), drop(g_out_attn), drop(g_out_sgu)
    w_sp1, b_sp1 = drop(w_sp), drop(b_sp)

    q_p, k_p, v_p, u_p, vs_p = _inproj(xp, mod_p, g_mix1, w_in_bf, g_sgu_flat, 512)
    q_s, k_s, v_s, u_s, vs_s = _inproj(xs, mod_s, g_mix1, w_in_bf, g_sgu_flat, tm_s)

    attn_p = _attn_prompt(q_p, k_p, v_p, _bias_prompt(rel_bias_table), batch, seq)
    qh_rows = dec_seq * N_HEADS
    assert qh_rows <= LANES
    per_head = lambda a: a.reshape(dec_batch, qh_rows, HEAD_DIM)
    pad_new = lambda a: jnp.pad(per_head(a), ((0, 0), (0, LANES - qh_rows), (0, 0)))
    page_rows = PAGE_SIZE * N_HEADS
    attn_s = _attn_sample(per_head(q_s), pad_new(k_s), pad_new(v_s),
                          cache_k.reshape(-1, page_rows, HEAD_DIM), cache_v.reshape(-1, page_rows, HEAD_DIM),
                          page_table, _bias_sample(rel_bias_table, past_len, dec_seq)).reshape(n_s, width)

    tri = jnp.tril(jnp.ones((SGU_CHUNK, SGU_CHUNK), F32))
    reps = SGU_CHUNK // dec_seq
    seq_id = jnp.arange(SGU_CHUNK) // dec_seq
    mask_s = tri * (seq_id[:, None] == seq_id[None, :]).astype(F32)
    wmix_s = jnp.tile(w_sp1[:, :dec_seq, :dec_seq], (1, reps, reps))
    bmix_p = jnp.repeat(b_sp1.T, SGU_WIDTH // N_GROUPS, axis=1)
    bmix_s = jnp.repeat(jnp.tile(b_sp1[:, :dec_seq], (1, reps)).T, SGU_WIDTH // N_GROUPS, axis=1)
    wr_pad = jnp.pad(drop(w_router), ((0, 0), (0, LANES - N_EXPERTS))).astype(BF16)
    br_pad = jnp.pad(drop(b_router), (0, LANES - N_EXPERTS), constant_values=NEG).reshape(1, LANES)

    assert n_p % tm_s == 0
    x1_p, h2_part, ti_p, tg_p = _mixout(attn_p, u_p, vs_p, xp, mod_p, w_sp1, tri, bmix_p, g_oa,
                                        g_os, w_out_bf, g_ffn1, wr_pad, br_pad, 256, n_p + n_s, 0)
    x1_s, h2, ti_s, tg_s = _mixout(attn_s, u_s, vs_s, xs, mod_s, wmix_s, mask_s, bmix_s, g_oa,
                                   g_os, w_out_bf, g_ffn1, wr_pad, br_pad, tm_s, n_p + n_s, n_p, h2_part)
    top_idx = jnp.concatenate([ti_p, ti_s], axis=0)[:, :TOP_K]
    dest, src_sorted, sb_expert, sb_rows, sb_off, sb_out = _route(top_idx)
    y_rows = _moe(h2, src_sorted, sb_expert, sb_rows, sb_off, sb_out,
                  drop(w_gate_up), drop(b_gate_up), drop(w_down), drop(b_down))

    y_p = _combine(x1_p, mod_p, y_rows, dest, tg_p, g_final, 256, 0)
    y_s = _combine(x1_s, mod_s, y_rows, dest, tg_s, g_final, tm_s, n_p)

    hd = (N_HEADS, HEAD_DIM)
    gd = (N_GROUPS, SGU_WIDTH // N_GROUPS)
    last_chunk = ((seq - 1) // SGU_CHUNK) * SGU_CHUNK
    return (y_p.reshape(batch, seq, d), y_s.reshape(dec_batch, dec_seq, d),
            k_p.reshape(1, batch, seq, *hd), v_p.reshape(1, batch, seq, *hd),
            k_s.reshape(1, dec_batch, dec_seq, *hd), v_s.reshape(1, dec_batch, dec_seq, *hd),
            vs_p.reshape(batch, seq, *gd)[:, last_chunk:][None], vs_s.reshape(1, dec_batch, dec_seq, *gd))
```
